```python
import jax, jax.numpy as jnp
from jax import lax
import numpy as np

D_MODEL = 2048
BATCH = 8
SEQ = 4096
DEPTH = 1
DEC_BATCH = 16
DEC_SEQ = 64
PAST_LEN = 2048

CHUNK = 64
D_A = D_MODEL // 2
DK_A = 128
H_A = D_A // DK_A
DV_A = D_A // H_A
D_B = D_MODEL // 2
H_B = 16
DH_B = D_B // H_B
BAND_CHUNKS = 8
BAND_ROWS = BAND_CHUNKS * CHUNK
REL_CLIP = 128
N_REL = 2 * REL_CLIP + 1
EPS = 1e-6
NEG_INF = -1e30
SPLITS = [int(s) for s in np.cumsum([D_A, D_A, D_A, D_A, D_B, D_B, D_B, D_B, D_MODEL])]
D_IN = 4 * D_A + 4 * D_B + 2 * D_MODEL

kernel_name = "hgrn2_chunkband_parallel_streaming_step"


def rms_norm(x, g):
    xf = x.astype(jnp.float32)
    y = xf * lax.rsqrt(jnp.mean(xf * xf, axis=-1, keepdims=True) + EPS)
    return (y * g.astype(jnp.float32)).astype(x.dtype)


def hgrn2_inputs(zq, zf, zi, lb):
    B, T, _ = zq.shape
    q = jax.nn.silu(zq.astype(jnp.float32)).reshape(B, T, H_A, DK_A)
    zf = zf.astype(jnp.float32).reshape(B, T, H_A, DK_A)
    lb = lb.reshape(H_A, DK_A)
    logf = jnp.log(lb + (1.0 - lb) * jax.nn.sigmoid(zf))
    k = (1.0 - lb) * jax.nn.sigmoid(-zf)
    v = zi.astype(jnp.float32).reshape(B, T, H_A, DV_A)
    return q, k, v, logf


def hgrn2_chunked(q, k, v, logf, s0, chunk_len):
    B, T, H, DK = q.shape
    DV = v.shape[-1]
    n = T // chunk_len
    tri = jnp.tril(jnp.ones((chunk_len, chunk_len), dtype=bool))[None, :, :, None, None]

    def to_chunks(a):
        return a.reshape(B, n, chunk_len, H, a.shape[-1]).swapaxes(0, 1)

    def step(S, inp):
        qc, kc, vc, gc = inp
        b = jnp.cumsum(gc, axis=1)
        o_inter = jnp.einsum('blhk,bhkv->blhv', qc * jnp.exp(b), S)
        diff = b[:, :, None] - b[:, None]
        decay = jnp.where(tri, jnp.exp(jnp.where(tri, diff, 0.0)), 0.0)
        scores = jnp.einsum('bthk,bshk,btshk->bhts', qc, kc, decay)
        o_intra = jnp.einsum('bhts,bshv->bthv', scores, vc)
        b_last = b[:, -1]
        S_new = jnp.exp(b_last)[..., None] * S + jnp.einsum(
            'bshk,bshv->bhkv', kc * jnp.exp(b_last[:, None] - b), vc)
        return S_new, o_inter + o_intra

    S_fin, o = lax.scan(step, s0.astype(jnp.float32),
                        (to_chunks(q), to_chunks(k), to_chunks(v), to_chunks(logf)))
    o = o.swapaxes(0, 1).reshape(B, T, H, DV)
    return o, S_fin


def band_attention(q, k, v, q_pos, k_pos, rel_bias):
    s = jnp.einsum('bqhd,bkhd->bhqk', q.astype(jnp.float32), k.astype(jnp.float32)) * (DH_B ** -0.5)
    rel = jnp.clip(q_pos[:, None] - k_pos[None, :], -REL_CLIP, REL_CLIP) + REL_CLIP
    s = s + rel_bias.astype(jnp.float32)[:, rel][None]
    qc = q_pos // CHUNK
    kc = k_pos // CHUNK
    allowed = (k_pos[None, :] >= 0) & (kc[None, :] <= qc[:, None]) & (kc[None, :] >= qc[:, None] - BAND_CHUNKS)
    s = jnp.where(allowed[None, None], s, NEG_INF)
    p = jax.nn.softmax(s, axis=-1)
    return jnp.einsum('bhqk,bkhd->bqhd', p, v.astype(jnp.float32))


def prompt_band_attention(q, k, v, rel_bias):
    B, T, H, DH = q.shape
    n = T // CHUNK
    pad = jnp.zeros((B, BAND_ROWS, H, DH), k.dtype)
    k_pad = jnp.concatenate([pad, k], axis=1)
    v_pad = jnp.concatenate([pad, v], axis=1)
    band = BAND_ROWS + CHUNK

    def one_chunk(c):
        start = c * CHUNK
        qc = lax.dynamic_slice_in_dim(q, start, CHUNK, axis=1)
        kc = lax.dynamic_slice_in_dim(k_pad, start, band, axis=1)
        vc = lax.dynamic_slice_in_dim(v_pad, start, band, axis=1)
        q_pos = start + jnp.arange(CHUNK, dtype=jnp.int32)
        k_pos = start - BAND_ROWS + jnp.arange(band, dtype=jnp.int32)
        return band_attention(qc, kc, vc, q_pos, k_pos, rel_bias)

    o = lax.map(one_chunk, jnp.arange(n, dtype=jnp.int32))
    return o.swapaxes(0, 1).reshape(B, T, H, DH)


def in_proj(x, norm_pre, w_in):
    z = rms_norm(x, norm_pre) @ w_in
    return jnp.split(z, SPLITS, axis=-1)


def merge_out(x, o_a, o_b, ga, gb, ma, mb, gnorm_a, w_proj_a, w_proj_b, w_out, norm_post):
    B, T, _ = x.shape
    o_a = o_a * lax.rsqrt(jnp.mean(o_a * o_a, axis=-1, keepdims=True) + EPS)
    o_a = o_a.reshape(B, T, D_A) * gnorm_a.astype(jnp.float32)
    o_a = (o_a * jax.nn.silu(ga.astype(jnp.float32))).astype(x.dtype)
    o_b = (o_b.reshape(B, T, D_B) * jax.nn.silu(gb.astype(jnp.float32))).astype(x.dtype)
    p_a = o_a @ w_proj_a
    p_b = o_b @ w_proj_b
    merged = (jax.nn.sigmoid(ma.astype(jnp.float32)) * p_a.astype(jnp.float32)
              + jax.nn.sigmoid(mb.astype(jnp.float32)) * p_b.astype(jnp.float32)).astype(x.dtype)
    return x + rms_norm(merged @ w_out, norm_post)


def layer_prompt(x, lb, norm_pre, w_in, gnorm_a, rel_bias, w_proj_a, w_proj_b, w_out, norm_post):
    B, T, _ = x.shape
    qa, fa, ia, ga, qb, kb, vb, gb, ma, mb = in_proj(x, norm_pre, w_in)
    q, k, v, logf = hgrn2_inputs(qa, fa, ia, lb)
    s0 = jnp.zeros((B, H_A, DK_A, DV_A), jnp.float32)
    o_a, s_fin = hgrn2_chunked(q, k, v, logf, s0, CHUNK)
    qh = qb.reshape(B, T, H_B, DH_B)
    kh = kb.reshape(B, T, H_B, DH_B)
    vh = vb.reshape(B, T, H_B, DH_B)
    o_b = prompt_band_attention(qh, kh, vh, rel_bias)
    y = merge_out(x, o_a, o_b, ga, gb, ma, mb, gnorm_a, w_proj_a, w_proj_b, w_out, norm_post)
    rows = min(BAND_ROWS, T)
    return y, s_fin, kh[:, T - rows:], vh[:, T - rows:]


def layer_sample(x, s0, past_k, past_v, lb, norm_pre, w_in, gnorm_a, rel_bias, w_proj_a, w_proj_b, w_out, norm_post):
    B, T, _ = x.shape
    qa, fa, ia, ga, qb, kb, vb, gb, ma, mb = in_proj(x, norm_pre, w_in)
    q, k, v, logf = hgrn2_inputs(qa, fa, ia, lb)
    o_a, s_fin = hgrn2_chunked(q, k, v, logf, s0, T)
    qh = qb.reshape(B, T, H_B, DH_B)
    kh = kb.reshape(B, T, H_B, DH_B)
    vh = vb.reshape(B, T, H_B, DH_B)
    rows = past_k.shape[1]
    k_all = jnp.concatenate([past_k.astype(kh.dtype), kh], axis=1)
    v_all = jnp.concatenate([past_v.astype(vh.dtype), vh], axis=1)
    q_pos = PAST_LEN + jnp.arange(T, dtype=jnp.int32)
    k_pos = jnp.concatenate([PAST_LEN - rows + jnp.arange(rows, dtype=jnp.int32), q_pos])
    o_b = band_attention(qh, k_all, v_all, q_pos, k_pos, rel_bias)
    y = merge_out(x, o_a, o_b, ga, gb, ma, mb, gnorm_a, w_proj_a, w_proj_b, w_out, norm_post)
    return y, s_fin, kh, vh


def setup_inputs(seed: int = 0) -> dict:
    key = jax.random.key(seed)
    ks = jax.random.split(key, 14)
    kv_rows = min(BAND_ROWS, PAST_LEN)
    f32 = jnp.float32
    return {
        "x_prompt": jax.random.normal(ks[0], (BATCH, SEQ, D_MODEL), f32),
        "x_sample": jax.random.normal(ks[1], (DEC_BATCH, DEC_SEQ, D_MODEL), f32),
        "state_hgrn": 0.3 * jax.random.normal(ks[2], (DEPTH, DEC_BATCH, H_A, DK_A, DV_A), f32),
        "cache_k": jax.random.normal(ks[3], (DEPTH, DEC_BATCH, kv_rows, H_B, DH_B), f32),
        "cache_v": jax.random.normal(ks[4], (DEPTH, DEC_BATCH, kv_rows, H_B, DH_B), f32),
        "norm_pre": 1.0 + 0.05 * jax.random.normal(ks[5], (DEPTH, D_MODEL), f32),
        "w_in": jax.random.normal(ks[6], (DEPTH, D_MODEL, D_IN), f32) * D_MODEL ** -0.5,
        "lb_logits": 0.5 * jax.random.normal(ks[7], (DEPTH + 1, D_A), f32),
        "gnorm_a": 1.0 + 0.05 * jax.random.normal(ks[8], (DEPTH, D_A), f32),
        "rel_bias": 0.1 * jax.random.normal(ks[9], (DEPTH, H_B, N_REL), f32),
        "w_proj_a": jax.random.normal(ks[10], (DEPTH, D_A, D_MODEL), f32) * D_A ** -0.5,
        "w_proj_b": jax.random.normal(ks[11], (DEPTH, D_B, D_MODEL), f32) * D_B ** -0.5,
        "w_out": jax.random.normal(ks[12], (DEPTH, D_MODEL, D_MODEL), f32) * D_MODEL ** -0.5,
        "norm_post": 1.0 + 0.05 * jax.random.normal(ks[13], (DEPTH, D_MODEL), f32),
    }


def reference(x_prompt, x_sample, state_hgrn, cache_k, cache_v, norm_pre, w_in, lb_logits,
              gnorm_a, rel_bias, w_proj_a, w_proj_b, w_out, norm_post):
    lb_all = jnp.cumsum(jax.nn.softmax(lb_logits.astype(jnp.float32), axis=0), axis=0)
    xp, xs = x_prompt, x_sample
    sp_list, kp_list, vp_list, ss_list, ks_list, vs_list = [], [], [], [], [], []
    for l in range(DEPTH):
        xp, sp, kp, vp = layer_prompt(xp, lb_all[l], norm_pre[l], w_in[l], gnorm_a[l], rel_bias[l],
                                      w_proj_a[l], w_proj_b[l], w_out[l], norm_post[l])
        xs, ss, ksn, vsn = layer_sample(xs, state_hgrn[l], cache_k[l], cache_v[l], lb_all[l], norm_pre[l],
                                        w_in[l], gnorm_a[l], rel_bias[l], w_proj_a[l], w_proj_b[l],
                                        w_out[l], norm_post[l])
        sp_list.append(sp); kp_list.append(kp); vp_list.append(vp)
        ss_list.append(ss); ks_list.append(ksn); vs_list.append(vsn)
    new_state_prompt = jnp.stack(sp_list)
    new_k_prompt = jnp.stack(kp_list)
    new_v_prompt = jnp.stack(vp_list)
    new_state_sample = jnp.stack(ss_list)
    new_k_sample = jnp.stack(ks_list)
    new_v_sample = jnp.stack(vs_list)
    return (xp, xs, new_state_prompt, new_k_prompt, new_v_prompt, new_state_sample, new_k_sample, new_v_sample)
```

```python
import functools

import numpy as np
import jax
import jax.numpy as jnp
from jax import lax
from jax.experimental import pallas as pl
from jax.experimental.pallas import tpu as pltpu

F32 = jnp.float32
BF16 = jnp.bfloat16

LANES = 128
CHUNK = 64
DK = 128
DH = 64
BAND_CHUNKS = 8
BAND_ROWS = BAND_CHUNKS * CHUNK
REL_CLIP = 128
PAST_LEN = 2048
N_REL = 2 * REL_CLIP + 1
EPS = 1e-6
NEG_INF = -1e30
QBLK = 2 * CHUNK
N_KBLK = BAND_ROWS // QBLK + 1
VMEM_LIMIT = 56 * 1024 * 1024


def _dot(a, b):
    return jnp.dot(a, b, preferred_element_type=F32)


def _dot_nt(a, b):
    return lax.dot_general(a, b, (((1,), (1,)), ((), ())), preferred_element_type=F32)


def _dot_tn(a, b):
    return lax.dot_general(a, b, (((0,), (0,)), ((), ())), preferred_element_type=F32)


def _sigmoid(x):
    return 1.0 / (1.0 + jnp.exp(-x))


def _in_proj_kernel(x_ref, g_ref, w_ref, z_ref, k_ref, v_ref, xn_ref, *, tiles_per_kv, j_k, j_v):
    i = pl.program_id(0)
    j = pl.program_id(1)

    @pl.when(j == 0)
    def _():
        x = x_ref[...]
        ms = jnp.mean(x * x, axis=-1, keepdims=True)
        xn_ref[...] = ((x * lax.rsqrt(ms + EPS)) * g_ref[...]).astype(BF16)

    acc = _dot(xn_ref[...], w_ref[...])
    z_ref[...] = acc.astype(BF16)
    is_kv_tile = (i % tiles_per_kv) == (tiles_per_kv - 1)

    @pl.when(jnp.logical_and(is_kv_tile, j == j_k))
    def _():
        k_ref[...] = acc

    @pl.when(jnp.logical_and(is_kv_tile, j == j_v))
    def _():
        v_ref[...] = acc


def _in_proj(x2, norm_pre, w_in_bf, *, rows_per_batch, kv_rows, tm=512, tn=1024):
    m, d = x2.shape
    d_in = w_in_bf.shape[1]
    d_b = (d_in - 2 * d) // 8
    assert d_b == tn and m % tm == 0
    if rows_per_batch >= tm:
        assert rows_per_batch % tm == 0 and kv_rows == tm
        tiles_per_kv = rows_per_batch // tm
    else:
        assert tm % rows_per_batch == 0 and kv_rows == rows_per_batch
        tiles_per_kv = 1
    n_kv = m // (tiles_per_kv * tm)
    kern = functools.partial(_in_proj_kernel, tiles_per_kv=tiles_per_kv, j_k=5, j_v=6)
    kv_spec = pl.BlockSpec((tm, tn), lambda i, j: (i // tiles_per_kv, 0))
    return pl.pallas_call(
        kern,
        grid=(m // tm, d_in // tn),
        in_specs=[
            pl.BlockSpec((tm, d), lambda i, j: (i, 0)),
            pl.BlockSpec((1, d), lambda i, j: (0, 0)),
            pl.BlockSpec((d, tn), lambda i, j: (0, j)),
        ],
        out_specs=[pl.BlockSpec((tm, tn), lambda i, j: (i, j)), kv_spec, kv_spec],
        out_shape=[
            jax.ShapeDtypeStruct((m, d_in), BF16),
            jax.ShapeDtypeStruct((n_kv * tm, tn), F32),
            jax.ShapeDtypeStruct((n_kv * tm, tn), F32),
        ],
        scratch_shapes=[pltpu.VMEM((tm, d), BF16)],
        compiler_params=pltpu.CompilerParams(
            dimension_semantics=("arbitrary", "arbitrary"), vmem_limit_bytes=VMEM_LIMIT),
        name="in_proj",
    )(x2, norm_pre.reshape(1, d), w_in_bf)


def _hgrn_sum_matrix():
    t = np.arange(CHUNK)[:, None]
    u = np.arange(CHUNK)[None, :]
    groups = [u <= t, u > t]
    for m in (8, 16, 32):
        ref = (t // (2 * m)) * 2 * m + m - 1
        groups.append(np.where(t > ref, (u > ref) & (u <= t), (u > t) & (u <= ref)))
    groups.append((u <= t) & (u // 8 == t // 8))
    return np.concatenate(groups, axis=0).astype(np.float32)


def _hgrn_chunk(zq, zf, v, lb, a_mat, st):
    q = zq * _sigmoid(zq)
    f = lb + (1.0 - lb) * _sigmoid(zf)
    logf = jnp.log(f)
    k = (1.0 - lb) * _sigmoid(-zf)

    hi = logf.astype(BF16)
    lo = (logf - hi.astype(F32)).astype(BF16)
    c2 = _dot(a_mat, jnp.concatenate([hi, lo], axis=1))
    c = c2[:, :DK] + c2[:, DK:]
    b = c[0:64]
    b_rev = c[64:128]
    c8 = c[320:384]

    o_inter = _dot_nt((q * jnp.exp(b)).astype(BF16), st.astype(BF16))
    kd = (k * jnp.exp(b_rev)).astype(BF16)
    st_new = st * jnp.exp(b[63:64, :]) + _dot_tn(v, kd)

    row = lax.broadcasted_iota(jnp.int32, (CHUNK, DK), 0)
    s_idx = lax.broadcasted_iota(jnp.int32, (CHUNK, CHUNK), 0)
    t_idx = lax.broadcasted_iota(jnp.int32, (CHUNK, CHUNK), 1)

    sc_t = None
    for lvl, m in enumerate((8, 16, 32)):
        zdec = jnp.exp(c[128 + 64 * lvl:192 + 64 * lvl])
        upper = (row % (2 * m)) >= m
        q_s = jnp.where(upper, q * zdec, 0.0).astype(BF16)
        k_s = jnp.where(upper, 0.0, k * zdec).astype(BF16)
        part = _dot_nt(k_s, q_s)
        if 2 * m < CHUNK:
            part = jnp.where((s_idx // (2 * m)) == (t_idx // (2 * m)), part, 0.0)
        sc_t = part if sc_t is None else sc_t + part

    sub = lax.broadcasted_iota(jnp.int32, (8, DK), 0)
    lane = lax.broadcasted_iota(jnp.int32, (8, DK), 1)
    diag_rows = []
    for blk in range(CHUNK // 8):
        kb = k[8 * blk:8 * blk + 8]
        cb = c8[8 * blk:8 * blk + 8]
        acc = jnp.zeros((8, DK), F32)
        for u in range(8):
            t = 8 * blk + u
            keep = sub <= u
            dec = jnp.exp(jnp.where(keep, c8[t:t + 1] - cb, 0.0))
            prod = jnp.where(keep, (q[t:t + 1] * kb) * dec, 0.0)
            col = jnp.sum(prod, axis=-1, keepdims=True)
            acc = jnp.where(lane == t, col, acc)
        diag_rows.append(acc)
    sc_t = sc_t + jnp.concatenate(diag_rows, axis=0)[:, :CHUNK]

    o = o_inter + _dot_tn(sc_t.astype(BF16), v)
    return o, st_new


def _hgrn_kernel(zq_ref, zf_ref, zi_ref, zg_ref, lbl_ref, gn_ref, s0_ref, a_ref,
                 o_ref, sfin_ref, st_ref, *, n_chunks):
    c_idx = pl.program_id(2)

    @pl.when(c_idx == 0)
    def _():
        st_ref[...] = s0_ref[0, 0].T

    lbl = lbl_ref[...]
    e = jnp.exp(lbl - jnp.max(lbl, axis=0, keepdims=True))
    lb = e[0:1] / jnp.sum(e, axis=0, keepdims=True)
    a_mat = a_ref[...]
    gn = gn_ref[...]

    def body(ci, st):
        r = pl.multiple_of(ci * CHUNK, CHUNK)
        zq = zq_ref[pl.ds(r, CHUNK), :].astype(F32)
        zf = zf_ref[pl.ds(r, CHUNK), :].astype(F32)
        v = zi_ref[pl.ds(r, CHUNK), :]
        zg = zg_ref[pl.ds(r, CHUNK), :].astype(F32)
        o, st_new = _hgrn_chunk(zq, zf, v, lb, a_mat, st)
        o = o * lax.rsqrt(jnp.mean(o * o, axis=-1, keepdims=True) + EPS)
        o = (o * gn) * (zg * _sigmoid(zg))
        o_ref[pl.ds(r, CHUNK), :] = o.astype(BF16)
        return st_new

    st = lax.fori_loop(0, n_chunks, body, st_ref[...])
    st_ref[...] = st

    @pl.when(c_idx == pl.num_programs(2) - 1)
    def _():
        sfin_ref[0, 0] = st.T


def _hgrn(z, lb_logits, gnorm, s0, *, batch, rows_per_batch, tc):
    m = z.shape[0]
    heads = s0.shape[1]
    nt = rows_per_batch // tc
    a_mat = jnp.asarray(_hgrn_sum_matrix(), BF16)

    def zspec(group):
        return pl.BlockSpec((tc, DK), lambda b, h, c: (b * nt + c, group * heads + h))

    kern = functools.partial(_hgrn_kernel, n_chunks=tc // CHUNK)
    return pl.pallas_call(
        kern,
        grid=(batch, heads, nt),
        in_specs=[
            zspec(0), zspec(1), zspec(2), zspec(3),
            pl.BlockSpec((lb_logits.shape[0], DK), lambda b, h, c: (0, h)),
            pl.BlockSpec((1, DK), lambda b, h, c: (0, h)),
            pl.BlockSpec((1, 1, DK, DK), lambda b, h, c: (b, h, 0, 0)),
            pl.BlockSpec(a_mat.shape, lambda b, h, c: (0, 0)),
        ],
        out_specs=[
            pl.BlockSpec((tc, DK), lambda b, h, c: (b * nt + c, h)),
            pl.BlockSpec((1, 1, DK, DK), lambda b, h, c: (b, h, 0, 0)),
        ],
        out_shape=[
            jax.ShapeDtypeStruct((m, heads * DK), BF16),
            jax.ShapeDtypeStruct(s0.shape, F32),
        ],
        scratch_shapes=[pltpu.VMEM((DK, DK), F32)],
        compiler_params=pltpu.CompilerParams(
            dimension_semantics=("arbitrary", "arbitrary", "arbitrary"), vmem_limit_bytes=VMEM_LIMIT),
        name="hgrn",
    )(z, z, z, z, lb_logits, gnorm.reshape(1, -1), s0, a_mat)


def _bias_kernel(rb_ref, o_ref):
    h = pl.program_id(0)
    i = lax.broadcasted_iota(jnp.int32, (QBLK, QBLK), 0)
    j = lax.broadcasted_iota(jnp.int32, (QBLK, QBLK), 1)
    qc = i // CHUNK
    for d in range(N_KBLK):
        rel = jnp.clip(i - j + QBLK * d, -REL_CLIP, REL_CLIP) + REL_CLIP
        kc = j // CHUNK - (QBLK // CHUNK) * d
        allowed = jnp.logical_and(kc <= qc, kc >= qc - BAND_CHUNKS)

        def body(r, acc):
            return jnp.where(rel == r, rb_ref[h, r], acc)

        tile = lax.fori_loop(0, N_REL, body, jnp.zeros((QBLK, QBLK), F32))
        o_ref[0, d] = jnp.where(allowed, tile, NEG_INF)


def _bias_table(rel_bias):
    heads = rel_bias.shape[0]
    return pl.pallas_call(
        _bias_kernel,
        grid=(heads,),
        in_specs=[pl.BlockSpec(memory_space=pltpu.SMEM)],
        out_specs=pl.BlockSpec((1, N_KBLK, QBLK, QBLK), lambda h: (h, 0, 0, 0)),
        out_shape=jax.ShapeDtypeStruct((heads, N_KBLK, QBLK, QBLK), F32),
        compiler_params=pltpu.CompilerParams(dimension_semantics=("arbitrary",)),
        name="rel_bias_table",
    )(rel_bias)


def _attn_kernel(q_ref, k_ref, v_ref, g_ref, bias_ref, o_ref, *, tq, koff):
    qi = pl.program_id(2)
    q2 = q_ref[...]
    lane = lax.broadcasted_iota(jnp.int32, (tq, LANES), 1)
    first = lane < DH
    scale = DH ** -0.5
    outs = []
    for hh in range(2):
        qm = jnp.where(first if hh == 0 else jnp.logical_not(first), q2, jnp.zeros_like(q2))
        s_tiles = []
        v_tiles = []
        for d in range(N_KBLK):
            r = qi * QBLK + koff - QBLK * d
            r_c = pl.multiple_of(jnp.maximum(r, 0), QBLK)
            s = _dot_nt(qm, k_ref[pl.ds(r_c, QBLK), :]) * scale + bias_ref[hh, d, 0:tq, :]
            s_tiles.append(jnp.where(r >= 0, s, NEG_INF))
            v_tiles.append(v_ref[pl.ds(r_c, QBLK), :])
        m_el = functools.reduce(jnp.maximum, s_tiles)
        m = jnp.max(m_el, axis=-1, keepdims=True)
        p_tiles = [jnp.exp(s - m) for s in s_tiles]
        l = jnp.sum(functools.reduce(jnp.add, p_tiles), axis=-1, keepdims=True)
        acc = None
        for p, vt in zip(p_tiles, v_tiles):
            part = _dot(p.astype(BF16), vt)
            acc = part if acc is None else acc + part
        outs.append(acc / l)
    o = jnp.where(first, outs[0], outs[1])
    g = g_ref[...].astype(F32)
    o_ref[...] = (o * (g * _sigmoid(g))).astype(BF16)


def _attention(q_arr, q_col, k_arr, k_col, v_arr, v_col, g_arr, g_col, bias, *,
               batch, q_rows, tq, koff):
    k_rows = k_arr.shape[0] // batch
    nq = q_rows // tq
    assert tq == QBLK or nq == 1
    pairs = bias.shape[0] // 2
    kern = functools.partial(_attn_kernel, tq=tq, koff=koff)
    return pl.pallas_call(
        kern,
        grid=(batch, pairs, nq),
        in_specs=[
            pl.BlockSpec((tq, LANES), lambda b, p, i: (b * nq + i, q_col + p)),
            pl.BlockSpec((k_rows, LANES), lambda b, p, i: (b, k_col + p)),
            pl.BlockSpec((k_rows, LANES), lambda b, p, i: (b, v_col + p)),
            pl.BlockSpec((tq, LANES), lambda b, p, i: (b * nq + i, g_col + p)),
            pl.BlockSpec((2, N_KBLK, QBLK, QBLK), lambda b, p, i: (p, 0, 0, 0)),
        ],
        out_specs=pl.BlockSpec((tq, LANES), lambda b, p, i: (b * nq + i, p)),
        out_shape=jax.ShapeDtypeStruct((batch * q_rows, pairs * LANES), BF16),
        compiler_params=pltpu.CompilerParams(
            dimension_semantics=("arbitrary", "arbitrary", "arbitrary"), vmem_limit_bytes=VMEM_LIMIT),
        name="band_attention",
    )(q_arr, k_arr, v_arr, g_arr, bias)


def _merge_out_kernel(oa_ref, ob_ref, ma_ref, mb_ref, x_ref, wpa_ref, wpb_ref, wo_ref, gp_ref, y_ref):
    pa = _dot(oa_ref[...], wpa_ref[...])
    pb = _dot(ob_ref[...], wpb_ref[...])
    merged = _sigmoid(ma_ref[...].astype(F32)) * pa + _sigmoid(mb_ref[...].astype(F32)) * pb
    hid = _dot(merged.astype(BF16), wo_ref[...])
    ms = jnp.mean(hid * hid, axis=-1, keepdims=True)
    y_ref[...] = x_ref[...] + (hid * lax.rsqrt(ms + EPS)) * gp_ref[...]


def _merge_out(o_a, o_b, z, x2, wpa_bf, wpb_bf, wo_bf, norm_post, *, tm=256):
    m, d = x2.shape
    d_a = o_a.shape[1]
    d_b = o_b.shape[1]
    ma_col = (4 * d_a + 4 * d_b) // d
    resident = dict(pipeline_mode=pl.Buffered(1))
    return pl.pallas_call(
        _merge_out_kernel,
        grid=(m // tm,),
        in_specs=[
            pl.BlockSpec((tm, d_a), lambda i: (i, 0)),
            pl.BlockSpec((tm, d_b), lambda i: (i, 0)),
            pl.BlockSpec((tm, d), lambda i: (i, ma_col)),
            pl.BlockSpec((tm, d), lambda i: (i, ma_col + 1)),
            pl.BlockSpec((tm, d), lambda i: (i, 0)),
            pl.BlockSpec((d_a, d), lambda i: (0, 0), **resident),
            pl.BlockSpec((d_b, d), lambda i: (0, 0), **resident),
            pl.BlockSpec((d, d), lambda i: (0, 0), **resident),
            pl.BlockSpec((1, d), lambda i: (0, 0)),
        ],
        out_specs=pl.BlockSpec((tm, d), lambda i: (i, 0)),
        out_shape=jax.ShapeDtypeStruct((m, d), F32),
        compiler_params=pltpu.CompilerParams(
            dimension_semantics=("arbitrary",), vmem_limit_bytes=VMEM_LIMIT),
        name="merge_out",
    )(o_a, o_b, z, z, x2, wpa_bf, wpb_bf, wo_bf, norm_post.reshape(1, d))


def kernel(x_prompt, x_sample, state_hgrn, cache_k, cache_v, norm_pre, w_in, lb_logits, gnorm_a,
           rel_bias, w_proj_a, w_proj_b, w_out, norm_post):
    depth = w_in.shape[0]
    assert depth == 1, "single-layer step"
    bp, tp, d = x_prompt.shape
    bs, ts, _ = x_sample.shape
    heads_a = state_hgrn.shape[2]
    heads_b = cache_k.shape[3]
    kv_rows = cache_k.shape[2]
    d_a = heads_a * DK
    d_b = heads_b * DH
    assert ts == CHUNK and kv_rows == BAND_ROWS and PAST_LEN % QBLK == 0 and tp % QBLK == 0

    w_in_bf = w_in[0].astype(BF16)
    wpa_bf = w_proj_a[0].astype(BF16)
    wpb_bf = w_proj_b[0].astype(BF16)
    wo_bf = w_out[0].astype(BF16)
    bias = _bias_table(rel_bias[0])
    col_b = 4 * d_a // LANES
    grp_b = d_b // LANES

    xp2 = x_prompt.reshape(bp * tp, d)
    rows_p = min(BAND_ROWS, tp)
    z_p, k_p, v_p = _in_proj(xp2, norm_pre[0], w_in_bf, rows_per_batch=tp, kv_rows=rows_p)
    oa_p, s_p = _hgrn(z_p, lb_logits, gnorm_a[0], jnp.zeros((bp, heads_a, DK, DK), F32),
                      batch=bp, rows_per_batch=tp, tc=512)
    ob_p = _attention(z_p, col_b, z_p, col_b + grp_b, z_p, col_b + 2 * grp_b, z_p, col_b + 3 * grp_b,
                      bias, batch=bp, q_rows=tp, tq=QBLK, koff=0)
    y_p = _merge_out(oa_p, ob_p, z_p, xp2, wpa_bf, wpb_bf, wo_bf, norm_post[0])

    xs2 = x_sample.reshape(bs * ts, d)
    z_s, k_s, v_s = _in_proj(xs2, norm_pre[0], w_in_bf, rows_per_batch=ts, kv_rows=ts)
    oa_s, s_s = _hgrn(z_s, lb_logits, gnorm_a[0], state_hgrn[0], batch=bs, rows_per_batch=ts, tc=ts)
    pad = jnp.zeros((bs, QBLK - ts, d_b), BF16)
    k_all = jnp.concatenate([cache_k[0].reshape(bs, kv_rows, d_b).astype(BF16),
                             k_s.reshape(bs, ts, d_b).astype(BF16), pad], axis=1)
    v_all = jnp.concatenate([cache_v[0].reshape(bs, kv_rows, d_b).astype(BF16),
                             v_s.reshape(bs, ts, d_b).astype(BF16), pad], axis=1)
    k_rows = kv_rows + QBLK
    ob_s = _attention(z_s, col_b, k_all.reshape(bs * k_rows, d_b), 0, v_all.reshape(bs * k_rows, d_b), 0,
                      z_s, col_b + 3 * grp_b, bias, batch=bs, q_rows=ts, tq=ts, koff=kv_rows)
    y_s = _merge_out(oa_s, ob_s, z_s, xs2, wpa_bf, wpb_bf, wo_bf, norm_post[0])

    return (
        y_p.reshape(bp, tp, d),
        y_s.reshape(bs, ts, d),
        s_p[None],
        k_p.reshape(1, bp, rows_p, heads_b, DH),
        v_p.reshape(1, bp, rows_p, heads_b, DH),
        s_s[None],
        k_s.reshape(1, bs, ts, heads_b, DH),
        v_s.reshape(1, bs, ts, heads_b, DH),
    )
```

```python
import functools

import numpy as np
import jax
import jax.numpy as jnp
from jax import lax
from jax.experimental import pallas as pl
from jax.experimental.pallas import tpu as pltpu

F32 = jnp.float32
BF16 = jnp.bfloat16

LANES = 128
CHUNK = 64
DK = 128
DH = 64
BAND_CHUNKS = 8
BAND_ROWS = BAND_CHUNKS * CHUNK
REL_CLIP = 128
PAST_LEN = 2048
N_REL = 2 * REL_CLIP + 1
EPS = 1e-6
NEG_INF = -1e30
QBLK = 2 * CHUNK
N_KBLK = BAND_ROWS // QBLK + 1
VMEM_LIMIT = 56 * 1024 * 1024


def _dot(a, b):
    return jnp.dot(a, b, preferred_element_type=F32)


def _dot_nt(a, b):
    return lax.dot_general(a, b, (((1,), (1,)), ((), ())), preferred_element_type=F32)


def _dot_tn(a, b):
    return lax.dot_general(a, b, (((0,), (0,)), ((), ())), preferred_element_type=F32)


def _sigmoid(x):
    return 1.0 / (1.0 + jnp.exp(-x))


def _in_proj_kernel(x_ref, g_ref, w_ref, z_ref, k_ref, v_ref, xn_ref, *, tiles_per_kv, j_k, j_v):
    i = pl.program_id(0)
    j = pl.program_id(1)

    @pl.when(j == 0)
    def _():
        x = x_ref[...]
        ms = jnp.mean(x * x, axis=-1, keepdims=True)
        xn_ref[...] = ((x * lax.rsqrt(ms + EPS)) * g_ref[...]).astype(BF16)

    acc = _dot(xn_ref[...], w_ref[...])
    z_ref[...] = acc.astype(BF16)
    is_kv_tile = (i % tiles_per_kv) == (tiles_per_kv - 1)

    @pl.when(jnp.logical_and(is_kv_tile, j == j_k))
    def _():
        k_ref[...] = acc

    @pl.when(jnp.logical_and(is_kv_tile, j == j_v))
    def _():
        v_ref[...] = acc


def _in_proj(x2, norm_pre, w_in_bf, *, rows_per_batch, kv_rows, tm=512, tn=1024):
    m, d = x2.shape
    d_in = w_in_bf.shape[1]
    d_b = (d_in - 2 * d) // 8
    assert d_b == tn and m % tm == 0
    if rows_per_batch >= tm:
        assert rows_per_batch % tm == 0 and kv_rows == tm
        tiles_per_kv = rows_per_batch // tm
    else:
        assert tm % rows_per_batch == 0 and kv_rows == rows_per_batch
        tiles_per_kv = 1
    n_kv = m // (tiles_per_kv * tm)
    kern = functools.partial(_in_proj_kernel, tiles_per_kv=tiles_per_kv, j_k=5, j_v=6)
    kv_spec = pl.BlockSpec((tm, tn), lambda i, j: (i // tiles_per_kv, 0))
    return pl.pallas_call(
        kern,
        grid=(m // tm, d_in // tn),
        in_specs=[
            pl.BlockSpec((tm, d), lambda i, j: (i, 0)),
            pl.BlockSpec((1, d), lambda i, j: (0, 0)),
            pl.BlockSpec((d, tn), lambda i, j: (0, j)),
        ],
        out_specs=[pl.BlockSpec((tm, tn), lambda i, j: (i, j)), kv_spec, kv_spec],
        out_shape=[
            jax.ShapeDtypeStruct((m, d_in), BF16),
            jax.ShapeDtypeStruct((n_kv * tm, tn), F32),
            jax.ShapeDtypeStruct((n_kv * tm, tn), F32),
        ],
        scratch_shapes=[pltpu.VMEM((tm, d), BF16)],
        compiler_params=pltpu.CompilerParams(
            dimension_semantics=("arbitrary", "arbitrary"), vmem_limit_bytes=VMEM_LIMIT),
        name="in_proj",
    )(x2, norm_pre.reshape(1, d), w_in_bf)


HGRN_LEVELS = (8, 16, 32)


def _hgrn_sum_matrix():
    t = np.arange(CHUNK)[:, None]
    u = np.arange(CHUNK)[None, :]
    groups = [u <= t, u > t]
    for m in HGRN_LEVELS:
        ref = (t // (2 * m)) * 2 * m + m - 1
        groups.append(np.where(t > ref, (u > ref) & (u <= t), (u > t) & (u <= ref)))
    groups.append((u <= t) & (u // 8 == t // 8))
    a = np.concatenate(groups, axis=0).astype(np.float32)
    return np.concatenate([a, a], axis=1)


def _hgrn_chunk(zq, zf, v, lb, oml, a2, st, block_masks, diag_masks, kpad_ref, cpad_ref):
    q = zq * _sigmoid(zq)
    sg = _sigmoid(zf)
    k = oml * (1.0 - sg)
    logf = jnp.log(lb + oml * sg)

    hi = logf.astype(BF16)
    lo = (logf - hi.astype(F32)).astype(BF16)
    c = _dot(a2, jnp.concatenate([hi, lo], axis=0))
    b = c[0:CHUNK]
    b_rev = c[CHUNK:2 * CHUNK]
    c8 = c[(2 + len(HGRN_LEVELS)) * CHUNK:]

    o_inter = _dot_nt((q * jnp.exp(b)).astype(BF16), st.astype(BF16))
    kd = (k * jnp.exp(b_rev)).astype(BF16)
    st_new = st * jnp.exp(b[CHUNK - 1:CHUNK, :]) + _dot_tn(v, kd)

    zero8 = jnp.zeros((8, DK), F32)
    sc = None
    for lvl, m in enumerate(HGRN_LEVELS):
        zdec = jnp.exp(c[(2 + lvl) * CHUNK:(3 + lvl) * CHUNK])
        q_rows, k_rows = [], []
        for g in range(CHUNK // 8):
            rows = slice(8 * g, 8 * g + 8)
            if (8 * g) % (2 * m) >= m:
                q_rows.append(q[rows] * zdec[rows])
                k_rows.append(zero8)
            else:
                q_rows.append(zero8)
                k_rows.append(k[rows] * zdec[rows])
        part = _dot_nt(jnp.concatenate(q_rows, axis=0).astype(BF16),
                       jnp.concatenate(k_rows, axis=0).astype(BF16))
        if 2 * m < CHUNK:
            part = jnp.where(block_masks[m], part, 0.0)
        sc = part if sc is None else sc + part

    kpad_ref[8:, :] = k
    cpad_ref[8:, :] = c8
    diag = []
    for blk in range(CHUNK // 8):
        r0 = 8 * blk
        qb = q[r0:r0 + 8]
        cb = c8[r0:r0 + 8]
        acc = jnp.where(diag_masks[0], jnp.sum(qb * k[r0:r0 + 8], axis=-1, keepdims=True), 0.0)
        for dl in range(1, 8):
            ks = kpad_ref[8 + r0 - dl:16 + r0 - dl, :]
            cs = cpad_ref[8 + r0 - dl:16 + r0 - dl, :]
            dec = jnp.exp(jnp.minimum(cb - cs, 0.0))
            col = jnp.sum((qb * ks) * dec, axis=-1, keepdims=True)
            acc = jnp.where(diag_masks[dl], col, acc)
        diag.append(pltpu.roll(acc, r0, axis=1) if blk else acc)
    sc = sc + jnp.concatenate(diag, axis=0)[:, :CHUNK]

    o = o_inter + _dot(sc.astype(BF16), v)
    return o, st_new


def _hgrn_kernel(zq_ref, zf_ref, zi_ref, zg_ref, lbl_ref, gn_ref, s0_ref, a_ref,
                 o_ref, sfin_ref, st_ref, kpad_ref, cpad_ref, *, n_chunks):
    c_idx = pl.program_id(2)

    @pl.when(c_idx == 0)
    def _():
        st_ref[...] = s0_ref[0, 0].T

    lbl = lbl_ref[...]
    e = jnp.exp(lbl - jnp.max(lbl, axis=0, keepdims=True))
    lb = e[0:1] / jnp.sum(e, axis=0, keepdims=True)
    oml = 1.0 - lb
    a2 = a_ref[...]
    gn = gn_ref[...]

    t_idx = lax.broadcasted_iota(jnp.int32, (CHUNK, CHUNK), 0)
    s_idx = lax.broadcasted_iota(jnp.int32, (CHUNK, CHUNK), 1)
    block_masks = {m: (t_idx // (2 * m)) == (s_idx // (2 * m)) for m in HGRN_LEVELS if 2 * m < CHUNK}
    sub = lax.broadcasted_iota(jnp.int32, (8, DK), 0)
    lane = lax.broadcasted_iota(jnp.int32, (8, DK), 1)
    diag_masks = [lane == sub - dl for dl in range(8)]

    kpad_ref[:, 0:8, :] = jnp.zeros((n_chunks, 8, DK), F32)
    cpad_ref[:, 0:8, :] = jnp.zeros((n_chunks, 8, DK), F32)

    st = st_ref[...]
    for ci in range(n_chunks):
        rows = slice(ci * CHUNK, (ci + 1) * CHUNK)
        zq = zq_ref[rows, :].astype(F32)
        zf = zf_ref[rows, :].astype(F32)
        zg = zg_ref[rows, :].astype(F32)
        o, st = _hgrn_chunk(zq, zf, zi_ref[rows, :], lb, oml, a2, st, block_masks, diag_masks,
                            kpad_ref.at[ci], cpad_ref.at[ci])
        o = o * lax.rsqrt(jnp.mean(o * o, axis=-1, keepdims=True) + EPS)
        o = (o * gn) * (zg * _sigmoid(zg))
        o_ref[rows, :] = o.astype(BF16)
    st_ref[...] = st

    @pl.when(c_idx == pl.num_programs(2) - 1)
    def _():
        sfin_ref[0, 0] = st.T


def _hgrn(z, lb_logits, gnorm, s0, *, batch, rows_per_batch, tc):
    m = z.shape[0]
    heads = s0.shape[1]
    nt = rows_per_batch // tc
    a_mat = jnp.asarray(_hgrn_sum_matrix(), BF16)

    def zspec(group):
        return pl.BlockSpec((tc, DK), lambda b, h, c: (b * nt + c, group * heads + h))

    kern = functools.partial(_hgrn_kernel, n_chunks=tc // CHUNK)
    return pl.pallas_call(
        kern,
        grid=(batch, heads, nt),
        in_specs=[
            zspec(0), zspec(1), zspec(2), zspec(3),
            pl.BlockSpec((lb_logits.shape[0], DK), lambda b, h, c: (0, h)),
            pl.BlockSpec((1, DK), lambda b, h, c: (0, h)),
            pl.BlockSpec((1, 1, DK, DK), lambda b, h, c: (b, h, 0, 0)),
            pl.BlockSpec(a_mat.shape, lambda b, h, c: (0, 0)),
        ],
        out_specs=[
            pl.BlockSpec((tc, DK), lambda b, h, c: (b * nt + c, h)),
            pl.BlockSpec((1, 1, DK, DK), lambda b, h, c: (b, h, 0, 0)),
        ],
        out_shape=[
            jax.ShapeDtypeStruct((m, heads * DK), BF16),
            jax.ShapeDtypeStruct(s0.shape, F32),
        ],
        scratch_shapes=[
            pltpu.VMEM((DK, DK), F32),
            pltpu.VMEM((tc // CHUNK, 8 + CHUNK, DK), F32),
            pltpu.VMEM((tc // CHUNK, 8 + CHUNK, DK), F32),
        ],
        compiler_params=pltpu.CompilerParams(
            dimension_semantics=("arbitrary", "arbitrary", "arbitrary"), vmem_limit_bytes=VMEM_LIMIT),
        name="hgrn",
    )(z, z, z, z, lb_logits, gnorm.reshape(1, -1), s0, a_mat)


LOG2E = 1.4426950408889634
BIAS_SPAN = 3 * QBLK


def _bias_kernel(rb_ref, o_ref):
    h = pl.program_id(0)
    l_idx = lax.broadcasted_iota(jnp.int32, (8, BIAS_SPAN), 1)
    src = jnp.minimum(BIAS_SPAN - 1 - l_idx, N_REL - 1)

    def body(r, acc):
        return jnp.where(src == r, rb_ref[h, r], acc)

    row = lax.fori_loop(0, N_REL, body, jnp.zeros((8, BIAS_SPAN), F32))
    wide = jnp.broadcast_to(row[0:1], (QBLK, BIAS_SPAN))
    rolled = pltpu.roll(wide, QBLK + 1, axis=1, stride=1, stride_axis=0)
    near = (rolled[:, 0:QBLK], rolled[:, 2 * QBLK:3 * QBLK])
    far = jnp.full((QBLK, QBLK), rb_ref[h, N_REL - 1], F32)

    i = lax.broadcasted_iota(jnp.int32, (QBLK, QBLK), 0)
    j = lax.broadcasted_iota(jnp.int32, (QBLK, QBLK), 1)
    qc = i // CHUNK
    for d in range(N_KBLK):
        kc = j // CHUNK - (QBLK // CHUNK) * d
        allowed = jnp.logical_and(kc <= qc, kc >= qc - BAND_CHUNKS)
        tile = near[d] if d < 2 else far
        o_ref[0, :, (N_KBLK - 1 - d) * QBLK:(N_KBLK - d) * QBLK] = jnp.where(allowed, tile * LOG2E, NEG_INF)


def _bias_table(rel_bias):
    heads = rel_bias.shape[0]
    assert rel_bias.shape[1] == N_REL and REL_CLIP == QBLK
    return pl.pallas_call(
        _bias_kernel,
        grid=(heads,),
        in_specs=[pl.BlockSpec(memory_space=pltpu.SMEM)],
        out_specs=pl.BlockSpec((1, QBLK, N_KBLK * QBLK), lambda h: (h, 0, 0)),
        out_shape=jax.ShapeDtypeStruct((heads, QBLK, N_KBLK * QBLK), F32),
        compiler_params=pltpu.CompilerParams(dimension_semantics=("arbitrary",)),
        name="rel_bias_table",
    )(rel_bias)


def _attn_block(q2, k_ref, v_ref, bias_ref, qi, *, tq, koff, clamp):
    lane = lax.broadcasted_iota(jnp.int32, (tq, LANES), 1)
    first = lane < DH
    r_lo = qi * QBLK + koff - BAND_ROWS
    outs = []
    for hh in range(2):
        qm = jnp.where(first if hh == 0 else jnp.logical_not(first), q2, jnp.zeros_like(q2))
        if clamp:
            s_tiles, v_tiles = [], []
            for t in range(N_KBLK):
                r = r_lo + QBLK * t
                r_c = pl.multiple_of(jnp.maximum(r, 0), QBLK)
                s = (_dot_nt(qm, k_ref[pl.ds(r_c, QBLK), :]) * (DH ** -0.5 * LOG2E)
                     + bias_ref[hh, 0:tq, QBLK * t:QBLK * (t + 1)])
                s_tiles.append(jnp.where(r >= 0, s, NEG_INF))
                v_tiles.append(v_ref[pl.ds(r_c, QBLK), :])
            m = jnp.max(functools.reduce(jnp.maximum, s_tiles), axis=-1, keepdims=True)
            p_tiles = [jnp.exp2(s - m) for s in s_tiles]
            l = jnp.sum(functools.reduce(jnp.add, p_tiles), axis=-1, keepdims=True)
            acc = functools.reduce(jnp.add, [_dot(p.astype(BF16), vt) for p, vt in zip(p_tiles, v_tiles)])
        else:
            rows = pl.ds(pl.multiple_of(r_lo, QBLK), N_KBLK * QBLK)
            s = _dot_nt(qm, k_ref[rows, :]) * (DH ** -0.5 * LOG2E) + bias_ref[hh, 0:tq, :]
            m = jnp.max(s, axis=-1, keepdims=True)
            p = jnp.exp2(s - m)
            l = jnp.sum(p, axis=-1, keepdims=True)
            acc = _dot(p.astype(BF16), v_ref[rows, :])
        outs.append(acc / l)
    return jnp.where(first, outs[0], outs[1])


def _attn_kernel(q_ref, k_ref, v_ref, g_ref, bias_ref, o_ref, *, tq, nqb, koff):
    step = pl.program_id(2)

    def compute(clamp):
        for qb in range(nqb):
            rows = slice(qb * tq, (qb + 1) * tq)
            o = _attn_block(q_ref[rows, :], k_ref, v_ref, bias_ref, step * nqb + qb,
                            tq=tq, koff=koff, clamp=clamp)
            g = g_ref[rows, :].astype(F32)
            o_ref[rows, :] = (o * (g * _sigmoid(g))).astype(BF16)

    full_from = max(-(-((N_KBLK - 1) * QBLK - koff) // (QBLK * nqb)), 0)
    if full_from == 0:
        compute(False)
    else:
        pl.when(step >= full_from)(lambda: compute(False))
        pl.when(step < full_from)(lambda: compute(True))


def _attention(q_arr, q_col, k_arr, k_col, v_arr, v_col, g_arr, g_col, bias, *,
               batch, q_rows, tq, nqb, koff):
    k_rows = k_arr.shape[0] // batch
    ts = tq * nqb
    nq = q_rows // ts
    assert q_rows % ts == 0 and (tq == QBLK or q_rows == tq)
    pairs = bias.shape[0] // 2
    kern = functools.partial(_attn_kernel, tq=tq, nqb=nqb, koff=koff)
    return pl.pallas_call(
        kern,
        grid=(batch, pairs, nq),
        in_specs=[
            pl.BlockSpec((ts, LANES), lambda b, p, i: (b * nq + i, q_col + p)),
            pl.BlockSpec((k_rows, LANES), lambda b, p, i: (b, k_col + p)),
            pl.BlockSpec((k_rows, LANES), lambda b, p, i: (b, v_col + p)),
            pl.BlockSpec((ts, LANES), lambda b, p, i: (b * nq + i, g_col + p)),
            pl.BlockSpec((2, QBLK, N_KBLK * QBLK), lambda b, p, i: (p, 0, 0)),
        ],
        out_specs=pl.BlockSpec((ts, LANES), lambda b, p, i: (b * nq + i, p)),
        out_shape=jax.ShapeDtypeStruct((batch * q_rows, pairs * LANES), BF16),
        compiler_params=pltpu.CompilerParams(
            dimension_semantics=("arbitrary", "arbitrary", "arbitrary"), vmem_limit_bytes=VMEM_LIMIT),
        name="band_attention",
    )(q_arr, k_arr, v_arr, g_arr, bias)


def _merge_out_kernel(oa_ref, ob_ref, ma_ref, mb_ref, x_ref, wpa_ref, wpb_ref, wo_ref, gp_ref, y_ref):
    pa = _dot(oa_ref[...], wpa_ref[...])
    pb = _dot(ob_ref[...], wpb_ref[...])
    merged = _sigmoid(ma_ref[...].astype(F32)) * pa + _sigmoid(mb_ref[...].astype(F32)) * pb
    hid = _dot(merged.astype(BF16), wo_ref[...])
    ms = jnp.mean(hid * hid, axis=-1, keepdims=True)
    y_ref[...] = x_ref[...] + (hid * lax.rsqrt(ms + EPS)) * gp_ref[...]


def _merge_out(o_a, o_b, z, x2, wpa_bf, wpb_bf, wo_bf, norm_post, *, tm=256):
    m, d = x2.shape
    d_a = o_a.shape[1]
    d_b = o_b.shape[1]
    ma_col = (4 * d_a + 4 * d_b) // d
    resident = dict(pipeline_mode=pl.Buffered(1))
    return pl.pallas_call(
        _merge_out_kernel,
        grid=(m // tm,),
        in_specs=[
            pl.BlockSpec((tm, d_a), lambda i: (i, 0)),
            pl.BlockSpec((tm, d_b), lambda i: (i, 0)),
            pl.BlockSpec((tm, d), lambda i: (i, ma_col)),
            pl.BlockSpec((tm, d), lambda i: (i, ma_col + 1)),
            pl.BlockSpec((tm, d), lambda i: (i, 0)),
            pl.BlockSpec((d_a, d), lambda i: (0, 0), **resident),
            pl.BlockSpec((d_b, d), lambda i: (0, 0), **resident),
            pl.BlockSpec((d, d), lambda i: (0, 0), **resident),
            pl.BlockSpec((1, d), lambda i: (0, 0)),
        ],
        out_specs=pl.BlockSpec((tm, d), lambda i: (i, 0)),
        out_shape=jax.ShapeDtypeStruct((m, d), F32),
        compiler_params=pltpu.CompilerParams(
            dimension_semantics=("arbitrary",), vmem_limit_bytes=VMEM_LIMIT),
        name="merge_out",
    )(o_a, o_b, z, z, x2, wpa_bf, wpb_bf, wo_bf, norm_post.reshape(1, d))


def kernel(x_prompt, x_sample, state_hgrn, cache_k, cache_v, norm_pre, w_in, lb_logits, gnorm_a,
           rel_bias, w_proj_a, w_proj_b, w_out, norm_post):
    depth = w_in.shape[0]
    assert depth == 1, "single-layer step"
    bp, tp, d = x_prompt.shape
    bs, ts, _ = x_sample.shape
    heads_a = state_hgrn.shape[2]
    heads_b = cache_k.shape[3]
    kv_rows = cache_k.shape[2]
    d_a = heads_a * DK
    d_b = heads_b * DH
    assert ts == CHUNK and kv_rows == BAND_ROWS and PAST_LEN % QBLK == 0 and tp % QBLK == 0

    w_in_bf = w_in[0].astype(BF16)
    wpa_bf = w_proj_a[0].astype(BF16)
    wpb_bf = w_proj_b[0].astype(BF16)
    wo_bf = w_out[0].astype(BF16)
    bias = _bias_table(rel_bias[0])
    col_b = 4 * d_a // LANES
    grp_b = d_b // LANES

    xp2 = x_prompt.reshape(bp * tp, d)
    rows_p = min(BAND_ROWS, tp)
    z_p, k_p, v_p = _in_proj(xp2, norm_pre[0], w_in_bf, rows_per_batch=tp, kv_rows=rows_p)
    oa_p, s_p = _hgrn(z_p, lb_logits, gnorm_a[0], jnp.zeros((bp, heads_a, DK, DK), F32),
                      batch=bp, rows_per_batch=tp, tc=1024)
    ob_p = _attention(z_p, col_b, z_p, col_b + grp_b, z_p, col_b + 2 * grp_b, z_p, col_b + 3 * grp_b,
                      bias, batch=bp, q_rows=tp, tq=QBLK, nqb=2, koff=0)
    y_p = _merge_out(oa_p, ob_p, z_p, xp2, wpa_bf, wpb_bf, wo_bf, norm_post[0])

    xs2 = x_sample.reshape(bs * ts, d)
    z_s, k_s, v_s = _in_proj(xs2, norm_pre[0], w_in_bf, rows_per_batch=ts, kv_rows=ts)
    oa_s, s_s = _hgrn(z_s, lb_logits, gnorm_a[0], state_hgrn[0], batch=bs, rows_per_batch=ts, tc=ts)
    pad = jnp.zeros((bs, QBLK - ts, d_b), BF16)
    k_all = jnp.concatenate([cache_k[0].reshape(bs, kv_rows, d_b).astype(BF16),
                             k_s.reshape(bs, ts, d_b).astype(BF16), pad], axis=1)
    v_all = jnp.concatenate([cache_v[0].reshape(bs, kv_rows, d_b).astype(BF16),
                             v_s.reshape(bs, ts, d_b).astype(BF16), pad], axis=1)
    k_rows = kv_rows + QBLK
    ob_s = _attention(z_s, col_b, k_all.reshape(bs * k_rows, d_b), 0, v_all.reshape(bs * k_rows, d_b), 0,
                      z_s, col_b + 3 * grp_b, bias, batch=bs, q_rows=ts, tq=ts, nqb=1, koff=kv_rows)
    y_s = _merge_out(oa_s, ob_s, z_s, xs2, wpa_bf, wpb_bf, wo_bf, norm_post[0])

    return (
        y_p.reshape(bp, tp, d),
        y_s.reshape(bs, ts, d),
        s_p[None],
        k_p.reshape(1, bp, rows_p, heads_b, DH),
        v_p.reshape(1, bp, rows_p, heads_b, DH),
        s_s[None],
        k_s.reshape(1, bs, ts, heads_b, DH),
        v_s.reshape(1, bs, ts, heads_b, DH),
    )
```

```python
import functools

import numpy as np
import jax
import jax.numpy as jnp
from jax import lax
from jax.experimental import pallas as pl
from jax.experimental.pallas import tpu as pltpu

F32 = jnp.float32
BF16 = jnp.bfloat16

LANES = 128
CHUNK = 64
DK = 128
DH = 64
BAND_CHUNKS = 8
BAND_ROWS = BAND_CHUNKS * CHUNK
REL_CLIP = 128
PAST_LEN = 2048
N_REL = 2 * REL_CLIP + 1
EPS = 1e-6
NEG_INF = -1e30
QBLK = 2 * CHUNK
N_KBLK = BAND_ROWS // QBLK + 1
VMEM_LIMIT = 56 * 1024 * 1024


def _dot(a, b):
    return jnp.dot(a, b, preferred_element_type=F32)


def _dot_nt(a, b):
    return lax.dot_general(a, b, (((1,), (1,)), ((), ())), preferred_element_type=F32)


def _dot_tn(a, b):
    return lax.dot_general(a, b, (((0,), (0,)), ((), ())), preferred_element_type=F32)


def _sigmoid(x):
    return 1.0 / (1.0 + jnp.exp(-x))


def _in_proj_kernel(x_ref, g_ref, w_ref, z_ref, k_ref, v_ref, xn_ref, *, tiles_per_kv, kv_tail, j_k, j_v):
    i = pl.program_id(0)
    j = pl.program_id(1)
    tm = x_ref.shape[0]

    @pl.when(j == 0)
    def _():
        x = x_ref[...]
        ms = jnp.mean(x * x, axis=-1, keepdims=True)
        xn_ref[...] = ((x * lax.rsqrt(ms + EPS)) * g_ref[...]).astype(BF16)

    acc = _dot(xn_ref[...], w_ref[...])
    z_ref[...] = acc.astype(BF16)
    is_kv_tile = (i % tiles_per_kv) == (tiles_per_kv - 1)

    @pl.when(jnp.logical_and(is_kv_tile, j == j_k))
    def _():
        k_ref[...] = acc[tm - kv_tail:, :]

    @pl.when(jnp.logical_and(is_kv_tile, j == j_v))
    def _():
        v_ref[...] = acc[tm - kv_tail:, :]


def _in_proj(x2, norm_pre, w_in_bf, *, rows_per_batch, kv_rows, tm=1024, tn=1024):
    m, d = x2.shape
    d_in = w_in_bf.shape[1]
    d_b = (d_in - 2 * d) // 8
    assert d_b == tn and m % tm == 0
    if rows_per_batch >= tm:
        assert rows_per_batch % tm == 0 and kv_rows <= tm
        tiles_per_kv = rows_per_batch // tm
        kv_tail = kv_rows
    else:
        assert tm % rows_per_batch == 0 and kv_rows == rows_per_batch
        tiles_per_kv = 1
        kv_tail = tm
    n_kv = m // (tiles_per_kv * tm)
    kern = functools.partial(_in_proj_kernel, tiles_per_kv=tiles_per_kv, kv_tail=kv_tail, j_k=5, j_v=6)
    kv_spec = pl.BlockSpec((kv_tail, tn), lambda i, j: (i // tiles_per_kv, 0))
    return pl.pallas_call(
        kern,
        grid=(m // tm, d_in // tn),
        in_specs=[
            pl.BlockSpec((tm, d), lambda i, j: (i, 0)),
            pl.BlockSpec((1, d), lambda i, j: (0, 0)),
            pl.BlockSpec((d, tn), lambda i, j: (0, j)),
        ],
        out_specs=[pl.BlockSpec((tm, tn), lambda i, j: (i, j)), kv_spec, kv_spec],
        out_shape=[
            jax.ShapeDtypeStruct((m, d_in), BF16),
            jax.ShapeDtypeStruct((n_kv * kv_tail, tn), F32),
            jax.ShapeDtypeStruct((n_kv * kv_tail, tn), F32),
        ],
        scratch_shapes=[pltpu.VMEM((tm, d), BF16)],
        compiler_params=pltpu.CompilerParams(
            dimension_semantics=("arbitrary", "arbitrary"), vmem_limit_bytes=VMEM_LIMIT),
        name="in_proj",
    )(x2, norm_pre.reshape(1, d), w_in_bf)


HGRN_LEVELS = (8, 16, 32)


def _hgrn_sum_matrix():
    t = np.arange(CHUNK)[:, None]
    u = np.arange(CHUNK)[None, :]
    groups = [u <= t, u > t]
    for m in HGRN_LEVELS:
        ref = (t // (2 * m)) * 2 * m + m - 1
        groups.append(np.where(t > ref, (u > ref) & (u <= t), (u > t) & (u <= ref)))
    groups.append((u <= t) & (u // 8 == t // 8))
    a = np.concatenate(groups, axis=0).astype(np.float32)
    return np.concatenate([a, a], axis=1)


def _hgrn_chunk(zq, zf, v, lb, oml, a2, st, block_masks, diag_masks, kpad_ref, cpad_ref):
    q = zq * _sigmoid(zq)
    sg = _sigmoid(zf)
    k = oml * (1.0 - sg)
    logf = jnp.log(lb + oml * sg)

    hi = logf.astype(BF16)
    lo = (logf - hi.astype(F32)).astype(BF16)
    c = _dot(a2, jnp.concatenate([hi, lo], axis=0))
    b = c[0:CHUNK]
    b_rev = c[CHUNK:2 * CHUNK]
    c8 = c[(2 + len(HGRN_LEVELS)) * CHUNK:]

    o_inter = _dot_nt((q * jnp.exp(b)).astype(BF16), st.astype(BF16))
    kd = (k * jnp.exp(b_rev)).astype(BF16)
    st_new = st * jnp.exp(b[CHUNK - 1:CHUNK, :]) + _dot_tn(v, kd)

    zero8 = jnp.zeros((8, DK), F32)
    sc = None
    for lvl, m in enumerate(HGRN_LEVELS):
        zdec = jnp.exp(c[(2 + lvl) * CHUNK:(3 + lvl) * CHUNK])
        q_rows, k_rows = [], []
        for g in range(CHUNK // 8):
            rows = slice(8 * g, 8 * g + 8)
            if (8 * g) % (2 * m) >= m:
                q_rows.append(q[rows] * zdec[rows])
                k_rows.append(zero8)
            else:
                q_rows.append(zero8)
                k_rows.append(k[rows] * zdec[rows])
        part = _dot_nt(jnp.concatenate(q_rows, axis=0).astype(BF16),
                       jnp.concatenate(k_rows, axis=0).astype(BF16))
        if 2 * m < CHUNK:
            part = jnp.where(block_masks[m], part, 0.0)
        sc = part if sc is None else sc + part

    kpad_ref[8:, :] = k
    cpad_ref[8:, :] = c8
    diag = []
    for blk in range(CHUNK // 8):
        r0 = 8 * blk
        qb = q[r0:r0 + 8]
        cb = c8[r0:r0 + 8]
        acc = jnp.where(diag_masks[0], jnp.sum(qb * k[r0:r0 + 8], axis=-1, keepdims=True), 0.0)
        for dl in range(1, 8):
            ks = kpad_ref[8 + r0 - dl:16 + r0 - dl, :]
            cs = cpad_ref[8 + r0 - dl:16 + r0 - dl, :]
            dec = jnp.exp(jnp.minimum(cb - cs, 0.0))
            col = jnp.sum((qb * ks) * dec, axis=-1, keepdims=True)
            acc = jnp.where(diag_masks[dl], col, acc)
        diag.append(pltpu.roll(acc, r0, axis=1) if blk else acc)
    sc = sc + jnp.concatenate(diag, axis=0)[:, :CHUNK]

    o = o_inter + _dot(sc.astype(BF16), v)
    return o, st_new


def _hgrn_kernel(zq_ref, zf_ref, zi_ref, zg_ref, lbl_ref, gn_ref, s0_ref, a_ref,
                 o_ref, sfin_ref, st_ref, kpad_ref, cpad_ref, *, n_chunks, hps):
    c_idx = pl.program_id(2)

    @pl.when(c_idx == 0)
    def _():
        for hh in range(hps):
            st_ref[hh] = s0_ref[0, hh].T

    a2 = a_ref[...]
    t_idx = lax.broadcasted_iota(jnp.int32, (CHUNK, CHUNK), 0)
    s_idx = lax.broadcasted_iota(jnp.int32, (CHUNK, CHUNK), 1)
    block_masks = {m: (t_idx // (2 * m)) == (s_idx // (2 * m)) for m in HGRN_LEVELS if 2 * m < CHUNK}
    sub = lax.broadcasted_iota(jnp.int32, (8, DK), 0)
    lane = lax.broadcasted_iota(jnp.int32, (8, DK), 1)
    diag_masks = [lane == sub - dl for dl in range(8)]

    kpad_ref[:, 0:8, :] = jnp.zeros((hps * n_chunks, 8, DK), F32)
    cpad_ref[:, 0:8, :] = jnp.zeros((hps * n_chunks, 8, DK), F32)

    for hh in range(hps):
        cols = slice(hh * DK, (hh + 1) * DK)
        lbl = lbl_ref[:, cols]
        e = jnp.exp(lbl - jnp.max(lbl, axis=0, keepdims=True))
        lb = e[0:1] / jnp.sum(e, axis=0, keepdims=True)
        oml = 1.0 - lb
        gn = gn_ref[:, cols]
        st = st_ref[hh]
        for ci in range(n_chunks):
            rows = slice(ci * CHUNK, (ci + 1) * CHUNK)
            zq = zq_ref[rows, cols].astype(F32)
            zf = zf_ref[rows, cols].astype(F32)
            zg = zg_ref[rows, cols].astype(F32)
            slot = hh * n_chunks + ci
            o, st = _hgrn_chunk(zq, zf, zi_ref[rows, cols], lb, oml, a2, st, block_masks, diag_masks,
                                kpad_ref.at[slot], cpad_ref.at[slot])
            o = o * lax.rsqrt(jnp.mean(o * o, axis=-1, keepdims=True) + EPS)
            o = (o * gn) * (zg * _sigmoid(zg))
            o_ref[rows, cols] = o.astype(BF16)
        st_ref[hh] = st

    @pl.when(c_idx == pl.num_programs(2) - 1)
    def _():
        for hh in range(hps):
            sfin_ref[0, hh] = st_ref[hh].T


def _hgrn(z, lb_logits, gnorm, s0, *, batch, rows_per_batch, tc, hps):
    m = z.shape[0]
    heads = s0.shape[1]
    nt = rows_per_batch // tc
    n_chunks = tc // CHUNK
    hg = heads // hps
    assert heads % hps == 0
    a_mat = jnp.asarray(_hgrn_sum_matrix(), BF16)
    w = DK * hps

    def zspec(group):
        return pl.BlockSpec((tc, w), lambda b, h, c: (b * nt + c, group * hg + h))

    kern = functools.partial(_hgrn_kernel, n_chunks=n_chunks, hps=hps)
    return pl.pallas_call(
        kern,
        grid=(batch, hg, nt),
        in_specs=[
            zspec(0), zspec(1), zspec(2), zspec(3),
            pl.BlockSpec((lb_logits.shape[0], w), lambda b, h, c: (0, h)),
            pl.BlockSpec((1, w), lambda b, h, c: (0, h)),
            pl.BlockSpec((1, hps, DK, DK), lambda b, h, c: (b, h, 0, 0)),
            pl.BlockSpec(a_mat.shape, lambda b, h, c: (0, 0)),
        ],
        out_specs=[
            pl.BlockSpec((tc, w), lambda b, h, c: (b * nt + c, h)),
            pl.BlockSpec((1, hps, DK, DK), lambda b, h, c: (b, h, 0, 0)),
        ],
        out_shape=[
            jax.ShapeDtypeStruct((m, heads * DK), BF16),
            jax.ShapeDtypeStruct(s0.shape, F32),
        ],
        scratch_shapes=[
            pltpu.VMEM((hps, DK, DK), F32),
            pltpu.VMEM((hps * n_chunks, 8 + CHUNK, DK), F32),
            pltpu.VMEM((hps * n_chunks, 8 + CHUNK, DK), F32),
        ],
        compiler_params=pltpu.CompilerParams(
            dimension_semantics=("arbitrary", "arbitrary", "arbitrary"), vmem_limit_bytes=VMEM_LIMIT),
        name="hgrn",
    )(z, z, z, z, lb_logits, gnorm.reshape(1, -1), s0, a_mat)


LOG2E = 1.4426950408889634
BIAS_SPAN = 3 * QBLK


def _bias_kernel(rb_ref, o_ref):
    h = pl.program_id(0)
    l_idx = lax.broadcasted_iota(jnp.int32, (8, BIAS_SPAN), 1)
    src = jnp.minimum(BIAS_SPAN - 1 - l_idx, N_REL - 1)

    def body(r, acc):
        return jnp.where(src == r, rb_ref[h, r], acc)

    row = lax.fori_loop(0, N_REL, body, jnp.zeros((8, BIAS_SPAN), F32))
    wide = jnp.broadcast_to(row[0:1], (QBLK, BIAS_SPAN))
    rolled = pltpu.roll(wide, QBLK + 1, axis=1, stride=1, stride_axis=0)
    near = (rolled[:, 0:QBLK], rolled[:, 2 * QBLK:3 * QBLK])
    far = jnp.full((QBLK, QBLK), rb_ref[h, N_REL - 1], F32)

    i = lax.broadcasted_iota(jnp.int32, (QBLK, QBLK), 0)
    j = lax.broadcasted_iota(jnp.int32, (QBLK, QBLK), 1)
    qc = i // CHUNK
    for d in range(N_KBLK):
        kc = j // CHUNK - (QBLK // CHUNK) * d
        allowed = jnp.logical_and(kc <= qc, kc >= qc - BAND_CHUNKS)
        tile = near[d] if d < 2 else far
        o_ref[0, :, (N_KBLK - 1 - d) * QBLK:(N_KBLK - d) * QBLK] = jnp.where(allowed, tile * LOG2E, NEG_INF)


def _bias_table(rel_bias):
    heads = rel_bias.shape[0]
    assert rel_bias.shape[1] == N_REL and REL_CLIP == QBLK
    return pl.pallas_call(
        _bias_kernel,
        grid=(heads,),
        in_specs=[pl.BlockSpec(memory_space=pltpu.SMEM)],
        out_specs=pl.BlockSpec((1, QBLK, N_KBLK * QBLK), lambda h: (h, 0, 0)),
        out_shape=jax.ShapeDtypeStruct((heads, QBLK, N_KBLK * QBLK), F32),
        compiler_params=pltpu.CompilerParams(dimension_semantics=("arbitrary",)),
        name="rel_bias_table",
    )(rel_bias)


ROW_GROUP = 16


def _attn_scores(qm, k_ref, cols_k, bias_ref, hh, r_lo, s_ref, *, tq, clamp):
    cs = DH ** -0.5 * LOG2E
    if clamp:
        for t in range(N_KBLK):
            r = r_lo + QBLK * t
            cols = slice(QBLK * t, QBLK * (t + 1))
            kt = k_ref[pl.ds(pl.multiple_of(jnp.maximum(r, 0), QBLK), QBLK), cols_k]
            s = _dot_nt(qm, kt) * cs + bias_ref[hh, 0:tq, cols]
            s_ref[0:tq, cols] = jnp.where(r >= 0, s, NEG_INF)
    else:
        rows = pl.ds(pl.multiple_of(r_lo, QBLK), N_KBLK * QBLK)
        s_ref[0:tq, :] = _dot_nt(qm, k_ref[rows, cols_k]) * cs + bias_ref[hh, 0:tq, :]


def _attn_softmax(s_ref, p_ref, *, tq):
    sums = []
    for g in range(tq // ROW_GROUP):
        rows = slice(g * ROW_GROUP, (g + 1) * ROW_GROUP)
        s = s_ref[rows, :]
        p = jnp.exp2(s - jnp.max(s, axis=-1, keepdims=True))
        sums.append(jnp.sum(p, axis=-1, keepdims=True))
        p_ref[rows, :] = p.astype(BF16)
    return jnp.concatenate(sums, axis=0)


def _attn_values(p_ref, v_ref, cols_v, r_lo, *, tq, clamp):
    if clamp:
        parts = []
        for t in range(N_KBLK):
            r = pl.multiple_of(jnp.maximum(r_lo + QBLK * t, 0), QBLK)
            parts.append(_dot(p_ref[0:tq, QBLK * t:QBLK * (t + 1)], v_ref[pl.ds(r, QBLK), cols_v]))
        return functools.reduce(jnp.add, parts)
    rows = pl.ds(pl.multiple_of(r_lo, QBLK), N_KBLK * QBLK)
    return _dot(p_ref[0:tq, :], v_ref[rows, cols_v])


def _attn_kernel(q_ref, k_ref, v_ref, g_ref, bias_ref, o_ref, *scratch, tq, nqb, pps, koff):
    step = pl.program_id(2)
    n_chain = 2 * nqb * pps
    s_refs = scratch[:n_chain]
    p_refs = scratch[n_chain:]
    lane = lax.broadcasted_iota(jnp.int32, (tq, LANES), 1)
    first = lane < DH

    def compute(clamp):
        chains = [(qb, pp, hh) for qb in range(nqb) for pp in range(pps) for hh in range(2)]
        r_los = [(step * nqb + qb) * QBLK + koff - BAND_ROWS for qb in range(nqb)]
        for c, (qb, pp, hh) in enumerate(chains):
            cols = slice(pp * LANES, (pp + 1) * LANES)
            q2 = q_ref[qb * tq:(qb + 1) * tq, cols]
            qm = jnp.where(first if hh == 0 else jnp.logical_not(first), q2, jnp.zeros_like(q2))
            _attn_scores(qm, k_ref, cols, bias_ref, 2 * pp + hh, r_los[qb], s_refs[c], tq=tq, clamp=clamp)
        sums = [_attn_softmax(s_refs[c], p_refs[c], tq=tq) for c in range(n_chain)]
        outs = [_attn_values(p_refs[c], v_ref, slice(pp * LANES, (pp + 1) * LANES), r_los[qb],
                             tq=tq, clamp=clamp) / sums[c]
                for c, (qb, pp, hh) in enumerate(chains)]
        for qb in range(nqb):
            for pp in range(pps):
                rows = slice(qb * tq, (qb + 1) * tq)
                cols = slice(pp * LANES, (pp + 1) * LANES)
                c0 = 2 * (qb * pps + pp)
                o = jnp.where(first, outs[c0], outs[c0 + 1])
                g = g_ref[rows, cols].astype(F32)
                o_ref[rows, cols] = (o * (g * _sigmoid(g))).astype(BF16)

    full_from = max(-(-(BAND_ROWS - koff) // (QBLK * nqb)), 0)
    if full_from == 0:
        compute(False)
    else:
        pl.when(step >= full_from)(lambda: compute(False))
        pl.when(step < full_from)(lambda: compute(True))


def _attention(q_arr, q_col, k_arr, k_col, v_arr, v_col, g_arr, g_col, bias, *,
               batch, q_rows, tq, nqb, pps, koff):
    k_rows = k_arr.shape[0] // batch
    ts = tq * nqb
    nq = q_rows // ts
    pairs = bias.shape[0] // 2
    assert q_rows % ts == 0 and (tq == QBLK or q_rows == tq) and pairs % pps == 0
    assert all(c % pps == 0 for c in (q_col, k_col, v_col, g_col))
    w = LANES * pps
    kern = functools.partial(_attn_kernel, tq=tq, nqb=nqb, pps=pps, koff=koff)
    return pl.pallas_call(
        kern,
        grid=(batch, pairs // pps, nq),
        in_specs=[
            pl.BlockSpec((ts, w), lambda b, p, i: (b * nq + i, q_col // pps + p)),
            pl.BlockSpec((k_rows, w), lambda b, p, i: (b, k_col // pps + p)),
            pl.BlockSpec((k_rows, w), lambda b, p, i: (b, v_col // pps + p)),
            pl.BlockSpec((ts, w), lambda b, p, i: (b * nq + i, g_col // pps + p)),
            pl.BlockSpec((2 * pps, QBLK, N_KBLK * QBLK), lambda b, p, i: (p, 0, 0)),
        ],
        out_specs=pl.BlockSpec((ts, w), lambda b, p, i: (b * nq + i, p)),
        out_shape=jax.ShapeDtypeStruct((batch * q_rows, pairs * LANES), BF16),
        scratch_shapes=([pltpu.VMEM((QBLK, N_KBLK * QBLK), F32)] * (2 * nqb * pps)
                        + [pltpu.VMEM((QBLK, N_KBLK * QBLK), BF16)] * (2 * nqb * pps)),
        compiler_params=pltpu.CompilerParams(
            dimension_semantics=("arbitrary", "arbitrary", "arbitrary"), vmem_limit_bytes=VMEM_LIMIT),
        name="band_attention",
    )(q_arr, k_arr, v_arr, g_arr, bias)


def _merge_out_kernel(oa_ref, ob_ref, ma_ref, mb_ref, x_ref, wpa_ref, wpb_ref, wo_ref, gp_ref, y_ref):
    pa = _dot(oa_ref[...], wpa_ref[...])
    pb = _dot(ob_ref[...], wpb_ref[...])
    merged = _sigmoid(ma_ref[...].astype(F32)) * pa + _sigmoid(mb_ref[...].astype(F32)) * pb
    hid = _dot(merged.astype(BF16), wo_ref[...])
    ms = jnp.mean(hid * hid, axis=-1, keepdims=True)
    y_ref[...] = x_ref[...] + (hid * lax.rsqrt(ms + EPS)) * gp_ref[...]


def _merge_out(o_a, o_b, z, x2, wpa_bf, wpb_bf, wo_bf, norm_post, *, tm=256):
    m, d = x2.shape
    d_a = o_a.shape[1]
    d_b = o_b.shape[1]
    ma_col = (4 * d_a + 4 * d_b) // d
    resident = dict(pipeline_mode=pl.Buffered(1))
    return pl.pallas_call(
        _merge_out_kernel,
        grid=(m // tm,),
        in_specs=[
            pl.BlockSpec((tm, d_a), lambda i: (i, 0)),
            pl.BlockSpec((tm, d_b), lambda i: (i, 0)),
            pl.BlockSpec((tm, d), lambda i: (i, ma_col)),
            pl.BlockSpec((tm, d), lambda i: (i, ma_col + 1)),
            pl.BlockSpec((tm, d), lambda i: (i, 0)),
            pl.BlockSpec((d_a, d), lambda i: (0, 0), **resident),
            pl.BlockSpec((d_b, d), lambda i: (0, 0), **resident),
            pl.BlockSpec((d, d), lambda i: (0, 0), **resident),
            pl.BlockSpec((1, d), lambda i: (0, 0)),
        ],
        out_specs=pl.BlockSpec((tm, d), lambda i: (i, 0)),
        out_shape=jax.ShapeDtypeStruct((m, d), F32),
        compiler_params=pltpu.CompilerParams(
            dimension_semantics=("arbitrary",), vmem_limit_bytes=VMEM_LIMIT),
        name="merge_out",
    )(o_a, o_b, z, z, x2, wpa_bf, wpb_bf, wo_bf, norm_post.reshape(1, d))


def kernel(x_prompt, x_sample, state_hgrn, cache_k, cache_v, norm_pre, w_in, lb_logits, gnorm_a,
           rel_bias, w_proj_a, w_proj_b, w_out, norm_post):
    depth = w_in.shape[0]
    assert depth == 1, "single-layer step"
    bp, tp, d = x_prompt.shape
    bs, ts, _ = x_sample.shape
    heads_a = state_hgrn.shape[2]
    heads_b = cache_k.shape[3]
    kv_rows = cache_k.shape[2]
    d_a = heads_a * DK
    d_b = heads_b * DH
    assert ts == CHUNK and kv_rows == BAND_ROWS and PAST_LEN % QBLK == 0 and tp % QBLK == 0

    w_in_bf = w_in[0].astype(BF16)
    wpa_bf = w_proj_a[0].astype(BF16)
    wpb_bf = w_proj_b[0].astype(BF16)
    wo_bf = w_out[0].astype(BF16)
    bias = _bias_table(rel_bias[0])
    col_b = 4 * d_a // LANES
    grp_b = d_b // LANES

    xp2 = x_prompt.reshape(bp * tp, d)
    rows_p = min(BAND_ROWS, tp)
    z_p, k_p, v_p = _in_proj(xp2, norm_pre[0], w_in_bf, rows_per_batch=tp, kv_rows=rows_p)
    oa_p, s_p = _hgrn(z_p, lb_logits, gnorm_a[0], jnp.zeros((bp, heads_a, DK, DK), F32),
                      batch=bp, rows_per_batch=tp, tc=1024, hps=1)
    ob_p = _attention(z_p, col_b, z_p, col_b + grp_b, z_p, col_b + 2 * grp_b, z_p, col_b + 3 * grp_b,
                      bias, batch=bp, q_rows=tp, tq=QBLK, nqb=4, pps=1, koff=0)
    y_p = _merge_out(oa_p, ob_p, z_p, xp2, wpa_bf, wpb_bf, wo_bf, norm_post[0])

    xs2 = x_sample.reshape(bs * ts, d)
    z_s, k_s, v_s = _in_proj(xs2, norm_pre[0], w_in_bf, rows_per_batch=ts, kv_rows=ts)
    oa_s, s_s = _hgrn(z_s, lb_logits, gnorm_a[0], state_hgrn[0], batch=bs, rows_per_batch=ts, tc=ts, hps=heads_a)
    pad = jnp.zeros((bs, QBLK - ts, d_b), BF16)
    k_all = jnp.concatenate([cache_k[0].astype(BF16).reshape(bs, kv_rows, d_b),
                             k_s.reshape(bs, ts, d_b).astype(BF16), pad], axis=1)
    v_all = jnp.concatenate([cache_v[0].astype(BF16).reshape(bs, kv_rows, d_b),
                             v_s.reshape(bs, ts, d_b).astype(BF16), pad], axis=1)
    k_rows = kv_rows + QBLK
    ob_s = _attention(z_s, col_b, k_all.reshape(bs * k_rows, d_b), 0, v_all.reshape(bs * k_rows, d_b), 0,
                      z_s, col_b + 3 * grp_b, bias, batch=bs, q_rows=ts, tq=ts, nqb=1, pps=4, koff=kv_rows)
    y_s = _merge_out(oa_s, ob_s, z_s, xs2, wpa_bf, wpb_bf, wo_bf, norm_post[0])

    return (
        y_p.reshape(bp, tp, d),
        y_s.reshape(bs, ts, d),
        s_p[None],
        k_p.reshape(1, bp, rows_p, heads_b, DH),
        v_p.reshape(1, bp, rows_p, heads_b, DH),
        s_s[None],
        k_s.reshape(1, bs, ts, heads_b, DH),
        v_s.reshape(1, bs, ts, heads_b, DH),
    )
```

```python
import functools

import numpy as np
import jax
import jax.numpy as jnp
from jax import lax
from jax.experimental import pallas as pl
from jax.experimental.pallas import tpu as pltpu

F32 = jnp.float32
BF16 = jnp.bfloat16

LANES = 128
CHUNK = 64
DK = 128
DH = 64
BAND_CHUNKS = 8
BAND_ROWS = BAND_CHUNKS * CHUNK
REL_CLIP = 128
PAST_LEN = 2048
N_REL = 2 * REL_CLIP + 1
EPS = 1e-6
NEG_INF = -1e30
QBLK = 2 * CHUNK
N_KBLK = BAND_ROWS // QBLK + 1
VMEM_LIMIT = 56 * 1024 * 1024
LOG2E = 1.4426950408889634
Q_SCALE = DH ** -0.5 * LOG2E


def _dot(a, b):
    return jnp.dot(a, b, preferred_element_type=F32)


def _dot_nt(a, b):
    return lax.dot_general(a, b, (((1,), (1,)), ((), ())), preferred_element_type=F32)


def _dot_tn(a, b):
    return lax.dot_general(a, b, (((0,), (0,)), ((), ())), preferred_element_type=F32)


def _sigmoid(x):
    return 1.0 / (1.0 + jnp.exp(-x))


def _in_proj_kernel(x_ref, g_ref, w_ref, z_ref, k_ref, v_ref, xn_ref, *, tiles_per_kv, kv_tail, j_q, j_k, j_v):
    i = pl.program_id(0)
    j = pl.program_id(1)
    tm = x_ref.shape[0]

    @pl.when(j == 0)
    def _():
        x = x_ref[...]
        ms = jnp.mean(x * x, axis=-1, keepdims=True)
        xn_ref[...] = ((x * lax.rsqrt(ms + EPS)) * g_ref[...]).astype(BF16)

    acc = _dot(xn_ref[...], w_ref[...])

    @pl.when(j == j_q)
    def _():
        z_ref[...] = (acc * Q_SCALE).astype(BF16)

    @pl.when(j != j_q)
    def _():
        z_ref[...] = acc.astype(BF16)

    is_kv_tile = (i % tiles_per_kv) == (tiles_per_kv - 1)

    @pl.when(jnp.logical_and(is_kv_tile, j == j_k))
    def _():
        k_ref[...] = acc[tm - kv_tail:, :]

    @pl.when(jnp.logical_and(is_kv_tile, j == j_v))
    def _():
        v_ref[...] = acc[tm - kv_tail:, :]


def _in_proj(x2, norm_pre, w_in_bf, *, rows_per_batch, kv_rows, tm=1024, tn=1024):
    m, d = x2.shape
    d_in = w_in_bf.shape[1]
    d_b = (d_in - 2 * d) // 8
    assert d_b == tn and m % tm == 0
    if rows_per_batch >= tm:
        assert rows_per_batch % tm == 0 and kv_rows <= tm
        tiles_per_kv = rows_per_batch // tm
        kv_tail = kv_rows
    else:
        assert tm % rows_per_batch == 0 and kv_rows == rows_per_batch
        tiles_per_kv = 1
        kv_tail = tm
    n_kv = m // (tiles_per_kv * tm)
    kern = functools.partial(_in_proj_kernel, tiles_per_kv=tiles_per_kv, kv_tail=kv_tail,
                             j_q=4 * d_b // tn, j_k=5 * d_b // tn, j_v=6 * d_b // tn)
    kv_spec = pl.BlockSpec((kv_tail, tn), lambda i, j: (i // tiles_per_kv, 0))
    return pl.pallas_call(
        kern,
        grid=(m // tm, d_in // tn),
        in_specs=[
            pl.BlockSpec((tm, d), lambda i, j: (i, 0)),
            pl.BlockSpec((1, d), lambda i, j: (0, 0)),
            pl.BlockSpec((d, tn), lambda i, j: (0, j)),
        ],
        out_specs=[pl.BlockSpec((tm, tn), lambda i, j: (i, j)), kv_spec, kv_spec],
        out_shape=[
            jax.ShapeDtypeStruct((m, d_in), BF16),
            jax.ShapeDtypeStruct((n_kv * kv_tail, tn), F32),
            jax.ShapeDtypeStruct((n_kv * kv_tail, tn), F32),
        ],
        scratch_shapes=[pltpu.VMEM((tm, d), BF16)],
        compiler_params=pltpu.CompilerParams(
            dimension_semantics=("arbitrary", "arbitrary"), vmem_limit_bytes=VMEM_LIMIT),
        name="in_proj",
    )(x2, norm_pre.reshape(1, d), w_in_bf)


HGRN_DIAG = 4
HGRN_LEVELS = tuple(m for m in (1, 2, 4, 8, 16, 32) if m >= HGRN_DIAG)


def _hgrn_sum_matrix():
    t = np.arange(CHUNK)[:, None]
    u = np.arange(CHUNK)[None, :]
    groups = [u <= t, u > t]
    for m in HGRN_LEVELS:
        ref = (t // (2 * m)) * 2 * m + m - 1
        groups.append(np.where(t > ref, (u > ref) & (u <= t), (u > t) & (u <= ref)))
    groups.append((u <= t) & (u // HGRN_DIAG == t // HGRN_DIAG))
    a = np.concatenate(groups, axis=0).astype(np.float32)
    return np.concatenate([a, a], axis=1)


def _hgrn_level_masks():
    t = np.arange(CHUNK)[:, None]
    s = np.arange(CHUNK)[None, :]
    masks = []
    for m in HGRN_LEVELS[:-1]:
        same = (t // (2 * m)) == (s // (2 * m))
        if m < 8:
            same = same & (t % (2 * m) >= m) & (s % (2 * m) < m)
        masks.append(same)
    return np.stack(masks).astype(np.float32)


def _hgrn_chunk(zq, zf, v, lb, oml, a2, st, level_masks, diag_masks, kpad_ref, cpad_ref):
    q = zq * _sigmoid(zq)
    sg = _sigmoid(zf)
    k = oml * (1.0 - sg)
    logf = jnp.log(lb + oml * sg)

    hi = logf.astype(BF16)
    lo = (logf - hi.astype(F32)).astype(BF16)
    c = _dot(a2, jnp.concatenate([hi, lo], axis=0))
    b = c[0:CHUNK]
    b_rev = c[CHUNK:2 * CHUNK]
    c_loc = c[(2 + len(HGRN_LEVELS)) * CHUNK:]

    o_inter = _dot_nt((q * jnp.exp(b)).astype(BF16), st.astype(BF16))
    kd = (k * jnp.exp(b_rev)).astype(BF16)
    st_new = st * jnp.exp(b[CHUNK - 1:CHUNK, :]) + _dot_tn(v, kd)

    zero8 = jnp.zeros((8, DK), F32)
    sc = None
    for lvl, m in enumerate(HGRN_LEVELS):
        zdec = jnp.exp(c[(2 + lvl) * CHUNK:(3 + lvl) * CHUNK])
        if m < 8:
            q_s = q * zdec
            k_s = k * zdec
        else:
            q_rows, k_rows = [], []
            for g in range(CHUNK // 8):
                rows = slice(8 * g, 8 * g + 8)
                if (8 * g) % (2 * m) >= m:
                    q_rows.append(q[rows] * zdec[rows])
                    k_rows.append(zero8)
                else:
                    q_rows.append(zero8)
                    k_rows.append(k[rows] * zdec[rows])
            q_s = jnp.concatenate(q_rows, axis=0)
            k_s = jnp.concatenate(k_rows, axis=0)
        part = _dot_nt(q_s.astype(BF16), k_s.astype(BF16))
        if m in level_masks:
            part = part * level_masks[m]
        sc = part if sc is None else sc + part

    kpad_ref[8:, :] = k
    cpad_ref[8:, :] = c_loc
    diag = []
    for blk in range(CHUNK // 8):
        r0 = 8 * blk
        qb = q[r0:r0 + 8]
        cb = c_loc[r0:r0 + 8]
        acc = jnp.where(diag_masks[0], jnp.sum(qb * k[r0:r0 + 8], axis=-1, keepdims=True), 0.0)
        for dl in range(1, HGRN_DIAG):
            ks = kpad_ref[8 + r0 - dl:16 + r0 - dl, :]
            cs = cpad_ref[8 + r0 - dl:16 + r0 - dl, :]
            dec = jnp.exp(jnp.minimum(cb - cs, 0.0))
            col = jnp.sum((qb * ks) * dec, axis=-1, keepdims=True)
            acc = jnp.where(diag_masks[dl], col, acc)
        diag.append(pltpu.roll(acc, r0, axis=1) if blk else acc)
    sc = sc + jnp.concatenate(diag, axis=0)[:, :CHUNK]

    o = o_inter + _dot(sc.astype(BF16), v)
    return o, st_new


def _hgrn_kernel(zq_ref, zf_ref, zi_ref, zg_ref, lbl_ref, gn_ref, s0_ref, a_ref, lm_ref,
                 o_ref, sfin_ref, st_ref, kpad_ref, cpad_ref, *, n_chunks, hps):
    c_idx = pl.program_id(2)

    @pl.when(c_idx == 0)
    def _():
        for hh in range(hps):
            st_ref[hh] = s0_ref[0, hh].T

    a2 = a_ref[...]
    level_masks = {m: lm_ref[i] for i, m in enumerate(HGRN_LEVELS[:-1])}
    sub = lax.broadcasted_iota(jnp.int32, (8, DK), 0)
    lane = lax.broadcasted_iota(jnp.int32, (8, DK), 1)
    diag_masks = [jnp.logical_and(lane == sub - dl, sub % HGRN_DIAG >= dl) for dl in range(HGRN_DIAG)]

    kpad_ref[:, 0:8, :] = jnp.zeros((hps * n_chunks, 8, DK), F32)
    cpad_ref[:, 0:8, :] = jnp.zeros((hps * n_chunks, 8, DK), F32)

    lbl = lbl_ref[...]
    e = jnp.exp(lbl - jnp.max(lbl, axis=0, keepdims=True))
    lb_all = e[0:1] / jnp.sum(e, axis=0, keepdims=True)

    state = [st_ref[hh] for hh in range(hps)]
    for slot, (hh, ci) in enumerate([(hh, ci) for hh in range(hps) for ci in range(n_chunks)]):
        rows, cols = slice(ci * CHUNK, (ci + 1) * CHUNK), slice(hh * DK, (hh + 1) * DK)
        lb = lb_all[:, cols]
        zg = zg_ref[rows, cols].astype(F32)
        o, state[hh] = _hgrn_chunk(zq_ref[rows, cols].astype(F32), zf_ref[rows, cols].astype(F32),
                                   zi_ref[rows, cols], lb, 1.0 - lb, a2, state[hh], level_masks,
                                   diag_masks, kpad_ref.at[slot], cpad_ref.at[slot])
        o = o * lax.rsqrt(jnp.mean(o * o, axis=-1, keepdims=True) + EPS)
        o_ref[rows, cols] = ((o * gn_ref[:, cols]) * (zg * _sigmoid(zg))).astype(BF16)
    for hh in range(hps):
        st_ref[hh] = state[hh]

    @pl.when(c_idx == pl.num_programs(2) - 1)
    def _():
        for hh in range(hps):
            sfin_ref[0, hh] = st_ref[hh].T


def _hgrn(z, lb_logits, gnorm, s0, *, batch, rows_per_batch, tc, hps):
    m = z.shape[0]
    heads = s0.shape[1]
    nt = rows_per_batch // tc
    n_chunks = tc // CHUNK
    hg = heads // hps
    assert heads % hps == 0
    a_mat = jnp.asarray(_hgrn_sum_matrix(), BF16)
    lmask = jnp.asarray(_hgrn_level_masks(), F32)
    w = DK * hps

    def zspec(group):
        return pl.BlockSpec((tc, w), lambda b, h, c: (b * nt + c, group * hg + h))

    kern = functools.partial(_hgrn_kernel, n_chunks=n_chunks, hps=hps)
    return pl.pallas_call(
        kern,
        grid=(batch, hg, nt),
        in_specs=[
            zspec(0), zspec(1), zspec(2), zspec(3),
            pl.BlockSpec((lb_logits.shape[0], w), lambda b, h, c: (0, h)),
            pl.BlockSpec((1, w), lambda b, h, c: (0, h)),
            pl.BlockSpec((1, hps, DK, DK), lambda b, h, c: (b, h, 0, 0)),
            pl.BlockSpec(a_mat.shape, lambda b, h, c: (0, 0)),
            pl.BlockSpec(lmask.shape, lambda b, h, c: (0, 0, 0)),
        ],
        out_specs=[
            pl.BlockSpec((tc, w), lambda b, h, c: (b * nt + c, h)),
            pl.BlockSpec((1, hps, DK, DK), lambda b, h, c: (b, h, 0, 0)),
        ],
        out_shape=[
            jax.ShapeDtypeStruct((m, heads * DK), BF16),
            jax.ShapeDtypeStruct(s0.shape, F32),
        ],
        scratch_shapes=[
            pltpu.VMEM((hps, DK, DK), F32),
            pltpu.VMEM((hps * n_chunks, 8 + CHUNK, DK), F32),
            pltpu.VMEM((hps * n_chunks, 8 + CHUNK, DK), F32),
        ],
        compiler_params=pltpu.CompilerParams(
            dimension_semantics=("arbitrary", "arbitrary", "arbitrary"), vmem_limit_bytes=VMEM_LIMIT),
        name="hgrn",
    )(z, z, z, z, lb_logits, gnorm.reshape(1, -1), s0, a_mat, lmask)


BIAS_SPAN = 3 * QBLK


def _bias_kernel(rb_ref, o_ref):
    h = pl.program_id(0)
    l_idx = lax.broadcasted_iota(jnp.int32, (8, BIAS_SPAN), 1)
    src = jnp.minimum(BIAS_SPAN - 1 - l_idx, N_REL - 1)

    def body(r, acc):
        return jnp.where(src == r, rb_ref[h, r], acc)

    row = lax.fori_loop(0, N_REL, body, jnp.zeros((8, BIAS_SPAN), F32))
    wide = jnp.broadcast_to(row[0:1], (QBLK, BIAS_SPAN))
    rolled = pltpu.roll(wide, QBLK + 1, axis=1, stride=1, stride_axis=0)
    near = (rolled[:, 0:QBLK], rolled[:, 2 * QBLK:3 * QBLK])
    far = jnp.full((QBLK, QBLK), rb_ref[h, N_REL - 1], F32)

    i = lax.broadcasted_iota(jnp.int32, (QBLK, QBLK), 0)
    j = lax.broadcasted_iota(jnp.int32, (QBLK, QBLK), 1)
    qc = i // CHUNK
    for d in range(N_KBLK):
        kc = j // CHUNK - (QBLK // CHUNK) * d
        allowed = jnp.logical_and(kc <= qc, kc >= qc - BAND_CHUNKS)
        tile = near[d] if d < 2 else far
        o_ref[0, :, (N_KBLK - 1 - d) * QBLK:(N_KBLK - d) * QBLK] = jnp.where(allowed, tile * LOG2E, NEG_INF)


def _bias_table(rel_bias):
    heads = rel_bias.shape[0]
    assert rel_bias.shape[1] == N_REL and REL_CLIP == QBLK
    return pl.pallas_call(
        _bias_kernel,
        grid=(heads,),
        in_specs=[pl.BlockSpec(memory_space=pltpu.SMEM)],
        out_specs=pl.BlockSpec((1, QBLK, N_KBLK * QBLK), lambda h: (h, 0, 0)),
        out_shape=jax.ShapeDtypeStruct((heads, QBLK, N_KBLK * QBLK), F32),
        compiler_params=pltpu.CompilerParams(dimension_semantics=("arbitrary",)),
        name="rel_bias_table",
    )(rel_bias)


ROW_GROUP = 16


def _attn_scores(qm, k_ref, cols_k, bias_ref, hh, r_lo, s_ref, *, tq, clamp):
    if clamp:
        for t in range(N_KBLK):
            r = r_lo + QBLK * t
            cols = slice(QBLK * t, QBLK * (t + 1))
            kt = k_ref[pl.ds(pl.multiple_of(jnp.maximum(r, 0), QBLK), QBLK), cols_k]
            s = _dot_nt(qm, kt) + bias_ref[hh, 0:tq, cols]
            s_ref[0:tq, cols] = jnp.where(r >= 0, s, NEG_INF)
    else:
        rows = pl.ds(pl.multiple_of(r_lo, QBLK), N_KBLK * QBLK)
        s_ref[0:tq, :] = _dot_nt(qm, k_ref[rows, cols_k]) + bias_ref[hh, 0:tq, :]


def _attn_softmax(s_ref, p_ref, *, tq):
    sums = []
    for g in range(tq // ROW_GROUP):
        rows = slice(g * ROW_GROUP, (g + 1) * ROW_GROUP)
        s = s_ref[rows, :]
        p = jnp.exp2(s - jnp.max(s, axis=-1, keepdims=True))
        sums.append(jnp.sum(p, axis=-1, keepdims=True))
        p_ref[rows, :] = p.astype(BF16)
    return jnp.concatenate(sums, axis=0)


def _attn_values(p_ref, v_ref, cols_v, r_lo, *, tq, clamp):
    if clamp:
        parts = []
        for t in range(N_KBLK):
            r = pl.multiple_of(jnp.maximum(r_lo + QBLK * t, 0), QBLK)
            parts.append(_dot(p_ref[0:tq, QBLK * t:QBLK * (t + 1)], v_ref[pl.ds(r, QBLK), cols_v]))
        return functools.reduce(jnp.add, parts)
    rows = pl.ds(pl.multiple_of(r_lo, QBLK), N_KBLK * QBLK)
    return _dot(p_ref[0:tq, :], v_ref[rows, cols_v])


def _attn_kernel(q_ref, k_ref, v_ref, g_ref, bias_ref, o_ref, *scratch, tq, nqb, pps, koff):
    step = pl.program_id(2)
    n_chain = 2 * nqb * pps
    s_refs = scratch[:n_chain]
    p_refs = scratch[n_chain:]
    lane = lax.broadcasted_iota(jnp.int32, (tq, LANES), 1)
    first = lane < DH

    def compute(clamp):
        chains = [(qb, pp, hh) for qb in range(nqb) for pp in range(pps) for hh in range(2)]
        r_los = [(step * nqb + qb) * QBLK + koff - BAND_ROWS for qb in range(nqb)]
        sums, outs = {}, {}
        for step_c in range(n_chain + 2):
            if step_c < n_chain:
                qb, pp, hh = chains[step_c]
                cols = slice(pp * LANES, (pp + 1) * LANES)
                q2 = q_ref[qb * tq:(qb + 1) * tq, cols]
                qm = jnp.where(first if hh == 0 else jnp.logical_not(first), q2, jnp.zeros_like(q2))
                _attn_scores(qm, k_ref, cols, bias_ref, 2 * pp + hh, r_los[qb], s_refs[step_c],
                             tq=tq, clamp=clamp)
            if 0 <= step_c - 1 < n_chain:
                sums[step_c - 1] = _attn_softmax(s_refs[step_c - 1], p_refs[step_c - 1], tq=tq)
            if 0 <= step_c - 2 < n_chain:
                c = step_c - 2
                qb, pp, hh = chains[c]
                outs[c] = _attn_values(p_refs[c], v_ref, slice(pp * LANES, (pp + 1) * LANES), r_los[qb],
                                       tq=tq, clamp=clamp) / sums[c]
        for qb in range(nqb):
            for pp in range(pps):
                rows = slice(qb * tq, (qb + 1) * tq)
                cols = slice(pp * LANES, (pp + 1) * LANES)
                c0 = 2 * (qb * pps + pp)
                o = jnp.where(first, outs[c0], outs[c0 + 1])
                g = g_ref[rows, cols].astype(F32)
                o_ref[rows, cols] = (o * (g * _sigmoid(g))).astype(BF16)

    full_from = max(-(-(BAND_ROWS - koff) // (QBLK * nqb)), 0)
    if full_from == 0:
        compute(False)
    else:
        pl.when(step >= full_from)(lambda: compute(False))
        pl.when(step < full_from)(lambda: compute(True))


def _attention(q_arr, q_col, k_arr, k_col, v_arr, v_col, g_arr, g_col, bias, *,
               batch, q_rows, tq, nqb, pps, koff):
    k_rows = k_arr.shape[0] // batch
    ts = tq * nqb
    nq = q_rows // ts
    pairs = bias.shape[0] // 2
    assert q_rows % ts == 0 and (tq == QBLK or q_rows == tq) and pairs % pps == 0
    assert all(c % pps == 0 for c in (q_col, k_col, v_col, g_col))
    w = LANES * pps
    kern = functools.partial(_attn_kernel, tq=tq, nqb=nqb, pps=pps, koff=koff)
    return pl.pallas_call(
        kern,
        grid=(batch, pairs // pps, nq),
        in_specs=[
            pl.BlockSpec((ts, w), lambda b, p, i: (b * nq + i, q_col // pps + p)),
            pl.BlockSpec((k_rows, w), lambda b, p, i: (b, k_col // pps + p)),
            pl.BlockSpec((k_rows, w), lambda b, p, i: (b, v_col // pps + p)),
            pl.BlockSpec((ts, w), lambda b, p, i: (b * nq + i, g_col // pps + p)),
            pl.BlockSpec((2 * pps, QBLK, N_KBLK * QBLK), lambda b, p, i: (p, 0, 0)),
        ],
        out_specs=pl.BlockSpec((ts, w), lambda b, p, i: (b * nq + i, p)),
        out_shape=jax.ShapeDtypeStruct((batch * q_rows, pairs * LANES), BF16),
        scratch_shapes=([pltpu.VMEM((QBLK, N_KBLK * QBLK), F32)] * (2 * nqb * pps)
                        + [pltpu.VMEM((QBLK, N_KBLK * QBLK), BF16)] * (2 * nqb * pps)),
        compiler_params=pltpu.CompilerParams(
            dimension_semantics=("arbitrary", "arbitrary", "arbitrary"), vmem_limit_bytes=VMEM_LIMIT),
        name="band_attention",
    )(q_arr, k_arr, v_arr, g_arr, bias)


def _merge_out_kernel(oa_ref, ob_ref, ma_ref, mb_ref, x_ref, wpa_ref, wpb_ref, wo_ref, gp_ref, y_ref):
    pa = _dot(oa_ref[...], wpa_ref[...])
    pb = _dot(ob_ref[...], wpb_ref[...])
    merged = _sigmoid(ma_ref[...].astype(F32)) * pa + _sigmoid(mb_ref[...].astype(F32)) * pb
    hid = _dot(merged.astype(BF16), wo_ref[...])
    ms = jnp.mean(hid * hid, axis=-1, keepdims=True)
    y_ref[...] = x_ref[...] + (hid * lax.rsqrt(ms + EPS)) * gp_ref[...]


def _merge_out(o_a, o_b, z, x2, wpa_bf, wpb_bf, wo_bf, norm_post, *, tm=256):
    m, d = x2.shape
    d_a = o_a.shape[1]
    d_b = o_b.shape[1]
    ma_col = (4 * d_a + 4 * d_b) // d
    resident = dict(pipeline_mode=pl.Buffered(1))
    return pl.pallas_call(
        _merge_out_kernel,
        grid=(m // tm,),
        in_specs=[
            pl.BlockSpec((tm, d_a), lambda i: (i, 0)),
            pl.BlockSpec((tm, d_b), lambda i: (i, 0)),
            pl.BlockSpec((tm, d), lambda i: (i, ma_col)),
            pl.BlockSpec((tm, d), lambda i: (i, ma_col + 1)),
            pl.BlockSpec((tm, d), lambda i: (i, 0)),
            pl.BlockSpec((d_a, d), lambda i: (0, 0), **resident),
            pl.BlockSpec((d_b, d), lambda i: (0, 0), **resident),
            pl.BlockSpec((d, d), lambda i: (0, 0), **resident),
            pl.BlockSpec((1, d), lambda i: (0, 0)),
        ],
        out_specs=pl.BlockSpec((tm, d), lambda i: (i, 0)),
        out_shape=jax.ShapeDtypeStruct((m, d), F32),
        compiler_params=pltpu.CompilerParams(
            dimension_semantics=("arbitrary",), vmem_limit_bytes=VMEM_LIMIT),
        name="merge_out",
    )(o_a, o_b, z, z, x2, wpa_bf, wpb_bf, wo_bf, norm_post.reshape(1, d))


def kernel(x_prompt, x_sample, state_hgrn, cache_k, cache_v, norm_pre, w_in, lb_logits, gnorm_a,
           rel_bias, w_proj_a, w_proj_b, w_out, norm_post):
    depth = w_in.shape[0]
    assert depth == 1, "single-layer step"
    bp, tp, d = x_prompt.shape
    bs, ts, _ = x_sample.shape
    heads_a = state_hgrn.shape[2]
    heads_b = cache_k.shape[3]
    kv_rows = cache_k.shape[2]
    d_a = heads_a * DK
    d_b = heads_b * DH
    assert ts == CHUNK and kv_rows == BAND_ROWS and PAST_LEN % QBLK == 0 and tp % QBLK == 0

    w_in_bf = w_in[0].astype(BF16)
    wpa_bf = w_proj_a[0].astype(BF16)
    wpb_bf = w_proj_b[0].astype(BF16)
    wo_bf = w_out[0].astype(BF16)
    bias = _bias_table(rel_bias[0])
    col_b = 4 * d_a // LANES
    grp_b = d_b // LANES

    xp2 = x_prompt.reshape(bp * tp, d)
    rows_p = min(BAND_ROWS, tp)
    z_p, k_p, v_p = _in_proj(xp2, norm_pre[0], w_in_bf, rows_per_batch=tp, kv_rows=rows_p)
    oa_p, s_p = _hgrn(z_p, lb_logits, gnorm_a[0], jnp.zeros((bp, heads_a, DK, DK), F32),
                      batch=bp, rows_per_batch=tp, tc=2048, hps=1)
    ob_p = _attention(z_p, col_b, z_p, col_b + grp_b, z_p, col_b + 2 * grp_b, z_p, col_b + 3 * grp_b,
                      bias, batch=bp, q_rows=tp, tq=QBLK, nqb=8, pps=1, koff=0)
    y_p = _merge_out(oa_p, ob_p, z_p, xp2, wpa_bf, wpb_bf, wo_bf, norm_post[0])

    xs2 = x_sample.reshape(bs * ts, d)
    z_s, k_s, v_s = _in_proj(xs2, norm_pre[0], w_in_bf, rows_per_batch=ts, kv_rows=ts)
    oa_s, s_s = _hgrn(z_s, lb_logits, gnorm_a[0], state_hgrn[0],
                      batch=bs, rows_per_batch=ts, tc=ts, hps=heads_a)
    pad = jnp.zeros((bs, QBLK - ts, d_b), BF16)
    k_all = jnp.concatenate([cache_k[0].astype(BF16).reshape(bs, kv_rows, d_b),
                             k_s.reshape(bs, ts, d_b).astype(BF16), pad], axis=1)
    v_all = jnp.concatenate([cache_v[0].astype(BF16).reshape(bs, kv_rows, d_b),
                             v_s.reshape(bs, ts, d_b).astype(BF16), pad], axis=1)
    k_rows = kv_rows + QBLK
    ob_s = _attention(z_s, col_b, k_all.reshape(bs * k_rows, d_b), 0, v_all.reshape(bs * k_rows, d_b), 0,
                      z_s, col_b + 3 * grp_b, bias, batch=bs, q_rows=ts, tq=ts, nqb=1, pps=4, koff=kv_rows)
    y_s = _merge_out(oa_s, ob_s, z_s, xs2, wpa_bf, wpb_bf, wo_bf, norm_post[0])

    return (
        y_p.reshape(bp, tp, d),
        y_s.reshape(bs, ts, d),
        s_p[None],
        k_p.reshape(1, bp, rows_p, heads_b, DH),
        v_p.reshape(1, bp, rows_p, heads_b, DH),
        s_s[None],
        k_s.reshape(1, bs, ts, heads_b, DH),
        v_s.reshape(1, bs, ts, heads_b, DH),
    )
```

```python
import functools

import numpy as np
import jax
import jax.numpy as jnp
from jax import lax
from jax.experimental import pallas as pl
from jax.experimental.pallas import tpu as pltpu

F32 = jnp.float32
BF16 = jnp.bfloat16

LANES = 128
CHUNK = 64
DK = 128
DH = 64
BAND_CHUNKS = 8
BAND_ROWS = BAND_CHUNKS * CHUNK
REL_CLIP = 128
PAST_LEN = 2048
N_REL = 2 * REL_CLIP + 1
EPS = 1e-6
NEG_INF = -1e30
QBLK = 2 * CHUNK
N_KBLK = BAND_ROWS // QBLK + 1
VMEM_LIMIT = 56 * 1024 * 1024
LOG2E = 1.4426950408889634
Q_SCALE = DH ** -0.5 * LOG2E


def _dot(a, b):
    return jnp.dot(a, b, preferred_element_type=F32)


def _dot_nt(a, b):
    return lax.dot_general(a, b, (((1,), (1,)), ((), ())), preferred_element_type=F32)


def _dot_tn(a, b):
    return lax.dot_general(a, b, (((0,), (0,)), ((), ())), preferred_element_type=F32)


def _sigmoid(x):
    return 1.0 / (1.0 + jnp.exp(-x))


def _in_proj_kernel(x_ref, g_ref, w_ref, z_ref, k_ref, v_ref, xn_ref, *, tiles_per_kv, kv_tail, j_q, j_k, j_v):
    i = pl.program_id(0)
    j = pl.program_id(1)
    tm = x_ref.shape[0]

    @pl.when(j == 0)
    def _():
        x = x_ref[...]
        ms = jnp.mean(x * x, axis=-1, keepdims=True)
        xn_ref[...] = ((x * lax.rsqrt(ms + EPS)) * g_ref[...]).astype(BF16)

    acc = _dot(xn_ref[...], w_ref[...])

    z_ref[...] = (acc * jnp.where(j == j_q, Q_SCALE, 1.0)).astype(BF16)
    is_kv_tile = (i % tiles_per_kv) == (tiles_per_kv - 1)

    @pl.when(jnp.logical_and(is_kv_tile, j == j_k))
    def _():
        k_ref[...] = acc[tm - kv_tail:, :]

    @pl.when(jnp.logical_and(is_kv_tile, j == j_v))
    def _():
        v_ref[...] = acc[tm - kv_tail:, :]


def _in_proj(x2, norm_pre, w_in_bf, *, rows_per_batch, kv_rows, tm=1024, tn=1024):
    m, d = x2.shape
    d_in = w_in_bf.shape[1]
    d_b = (d_in - 2 * d) // 8
    assert d_b == tn and m % tm == 0
    if rows_per_batch >= tm:
        assert rows_per_batch % tm == 0 and kv_rows <= tm
        tiles_per_kv = rows_per_batch // tm
        kv_tail = kv_rows
    else:
        assert tm % rows_per_batch == 0 and kv_rows == rows_per_batch
        tiles_per_kv = 1
        kv_tail = tm
    n_kv = m // (tiles_per_kv * tm)
    kern = functools.partial(_in_proj_kernel, tiles_per_kv=tiles_per_kv, kv_tail=kv_tail,
                             j_q=4 * d_b // tn, j_k=5 * d_b // tn, j_v=6 * d_b // tn)
    kv_spec = pl.BlockSpec((kv_tail, tn), lambda i, j: (i // tiles_per_kv, 0))
    return pl.pallas_call(
        kern,
        grid=(m // tm, d_in // tn),
        in_specs=[
            pl.BlockSpec((tm, d), lambda i, j: (i, 0)),
            pl.BlockSpec((1, d), lambda i, j: (0, 0)),
            pl.BlockSpec((d, tn), lambda i, j: (0, j)),
        ],
        out_specs=[pl.BlockSpec((tm, tn), lambda i, j: (i, j)), kv_spec, kv_spec],
        out_shape=[
            jax.ShapeDtypeStruct((m, d_in), BF16),
            jax.ShapeDtypeStruct((n_kv * kv_tail, tn), F32),
            jax.ShapeDtypeStruct((n_kv * kv_tail, tn), F32),
        ],
        scratch_shapes=[pltpu.VMEM((tm, d), BF16)],
        compiler_params=pltpu.CompilerParams(
            dimension_semantics=("arbitrary", "arbitrary"), vmem_limit_bytes=VMEM_LIMIT),
        name="in_proj",
    )(x2, norm_pre.reshape(1, d), w_in_bf)


HGRN_DIAG = 4
HGRN_LEVELS = tuple(m for m in (1, 2, 4, 8, 16, 32) if m >= HGRN_DIAG)


def _hgrn_sum_matrix():
    t = np.arange(CHUNK)[:, None]
    u = np.arange(CHUNK)[None, :]
    groups = [u <= t, u > t]
    for m in HGRN_LEVELS:
        ref = (t // (2 * m)) * 2 * m + m - 1
        groups.append(np.where(t > ref, (u > ref) & (u <= t), (u > t) & (u <= ref)))
    groups.append((u <= t) & (u // HGRN_DIAG == t // HGRN_DIAG))
    a = np.concatenate(groups, axis=0).astype(np.float32)
    return np.concatenate([a, a], axis=1)


def _hgrn_level_masks():
    t = np.arange(CHUNK)[:, None]
    s = np.arange(CHUNK)[None, :]
    masks = []
    for m in HGRN_LEVELS[:-1]:
        same = (t // (2 * m)) == (s // (2 * m))
        if m < 8:
            same = same & (t % (2 * m) >= m) & (s % (2 * m) < m)
        masks.append(same)
    return np.stack(masks).astype(np.float32)


def _hgrn_chunk(zq, zf, v, lb, oml, a2, st, level_masks, diag_masks, kpad_ref, cpad_ref):
    q = zq * _sigmoid(zq)
    sg = _sigmoid(zf)
    k = oml * (1.0 - sg)
    logf = jnp.log2(lb + oml * sg)

    hi = logf.astype(BF16)
    lo = (logf - hi.astype(F32)).astype(BF16)
    c = _dot(a2, jnp.concatenate([hi, lo], axis=0))
    b = c[0:CHUNK]
    b_rev = c[CHUNK:2 * CHUNK]
    c_loc = c[(2 + len(HGRN_LEVELS)) * CHUNK:]

    o_inter = _dot_nt((q * jnp.exp2(b)).astype(BF16), st.astype(BF16))
    kd = (k * jnp.exp2(b_rev)).astype(BF16)
    st_new = st * jnp.exp2(b[CHUNK - 1:CHUNK, :]) + _dot_tn(v, kd)

    zero8 = jnp.zeros((8, DK), F32)
    sc = None
    for lvl, m in enumerate(HGRN_LEVELS):
        zdec = jnp.exp2(c[(2 + lvl) * CHUNK:(3 + lvl) * CHUNK])
        if m < 8:
            q_s = q * zdec
            k_s = k * zdec
        else:
            q_rows, k_rows = [], []
            for g in range(CHUNK // 8):
                rows = slice(8 * g, 8 * g + 8)
                if (8 * g) % (2 * m) >= m:
                    q_rows.append(q[rows] * zdec[rows])
                    k_rows.append(zero8)
                else:
                    q_rows.append(zero8)
                    k_rows.append(k[rows] * zdec[rows])
            q_s = jnp.concatenate(q_rows, axis=0)
            k_s = jnp.concatenate(k_rows, axis=0)
        part = _dot_nt(q_s.astype(BF16), k_s.astype(BF16))
        if m in level_masks:
            part = part * level_masks[m]
        sc = part if sc is None else sc + part

    kpad_ref[8:, :] = k
    cpad_ref[8:, :] = c_loc
    diag = []
    for blk in range(CHUNK // 8):
        r0 = 8 * blk
        qb = q[r0:r0 + 8]
        cb = c_loc[r0:r0 + 8]
        acc = jnp.where(diag_masks[0], jnp.sum(qb * k[r0:r0 + 8], axis=-1, keepdims=True), 0.0)
        for dl in range(1, HGRN_DIAG):
            ks = kpad_ref[8 + r0 - dl:16 + r0 - dl, :]
            cs = cpad_ref[8 + r0 - dl:16 + r0 - dl, :]
            dec = jnp.exp2(jnp.minimum(cb - cs, 0.0))
            col = jnp.sum((qb * ks) * dec, axis=-1, keepdims=True)
            acc = jnp.where(diag_masks[dl], col, acc)
        diag.append(pltpu.roll(acc, r0, axis=1) if blk else acc)
    sc = sc + jnp.concatenate(diag, axis=0)[:, :CHUNK]

    o = o_inter + _dot(sc.astype(BF16), v)
    return o, st_new


def _hgrn_kernel(zq_ref, zf_ref, zi_ref, zg_ref, lbl_ref, gn_ref, s0_ref, a_ref, lm_ref,
                 o_ref, sfin_ref, st_ref, kpad_ref, cpad_ref, *, n_chunks, hps):
    c_idx = pl.program_id(2)

    @pl.when(c_idx == 0)
    def _():
        for hh in range(hps):
            st_ref[hh] = s0_ref[0, hh].T

    a2 = a_ref[...]
    level_masks = {m: lm_ref[i] for i, m in enumerate(HGRN_LEVELS[:-1])}
    sub = lax.broadcasted_iota(jnp.int32, (8, DK), 0)
    lane = lax.broadcasted_iota(jnp.int32, (8, DK), 1)
    diag_masks = [jnp.logical_and(lane == sub - dl, sub % HGRN_DIAG >= dl) for dl in range(HGRN_DIAG)]

    kpad_ref[:, 0:8, :] = jnp.zeros((hps * n_chunks, 8, DK), F32)
    cpad_ref[:, 0:8, :] = jnp.zeros((hps * n_chunks, 8, DK), F32)

    lbl = lbl_ref[...]
    e = jnp.exp(lbl - jnp.max(lbl, axis=0, keepdims=True))
    lb_all = e[0:1] / jnp.sum(e, axis=0, keepdims=True)

    state = [st_ref[hh] for hh in range(hps)]
    for slot, (hh, ci) in enumerate([(hh, ci) for hh in range(hps) for ci in range(n_chunks)]):
        rows, cols = slice(ci * CHUNK, (ci + 1) * CHUNK), slice(hh * DK, (hh + 1) * DK)
        lb = lb_all[:, cols]
        zg = zg_ref[rows, cols].astype(F32)
        o, state[hh] = _hgrn_chunk(zq_ref[rows, cols].astype(F32), zf_ref[rows, cols].astype(F32),
                                   zi_ref[rows, cols], lb, 1.0 - lb, a2, state[hh], level_masks,
                                   diag_masks, kpad_ref.at[slot], cpad_ref.at[slot])
        o = o * lax.rsqrt(jnp.mean(o * o, axis=-1, keepdims=True) + EPS)
        o_ref[rows, cols] = ((o * gn_ref[:, cols]) * (zg * _sigmoid(zg))).astype(BF16)
    for hh in range(hps):
        st_ref[hh] = state[hh]

    @pl.when(c_idx == pl.num_programs(2) - 1)
    def _():
        for hh in range(hps):
            sfin_ref[0, hh] = st_ref[hh].T


def _hgrn(z, lb_logits, gnorm, s0, *, batch, rows_per_batch, tc, hps):
    m = z.shape[0]
    heads = s0.shape[1]
    nt = rows_per_batch // tc
    n_chunks = tc // CHUNK
    hg = heads // hps
    assert heads % hps == 0
    a_mat = jnp.asarray(_hgrn_sum_matrix(), BF16)
    lmask = jnp.asarray(_hgrn_level_masks(), F32)
    w = DK * hps

    def zspec(group):
        return pl.BlockSpec((tc, w), lambda b, h, c: (b * nt + c, group * hg + h))

    kern = functools.partial(_hgrn_kernel, n_chunks=n_chunks, hps=hps)
    return pl.pallas_call(
        kern,
        grid=(batch, hg, nt),
        in_specs=[
            zspec(0), zspec(1), zspec(2), zspec(3),
            pl.BlockSpec((lb_logits.shape[0], w), lambda b, h, c: (0, h)),
            pl.BlockSpec((1, w), lambda b, h, c: (0, h)),
            pl.BlockSpec((1, hps, DK, DK), lambda b, h, c: (b, h, 0, 0)),
            pl.BlockSpec(a_mat.shape, lambda b, h, c: (0, 0)),
            pl.BlockSpec(lmask.shape, lambda b, h, c: (0, 0, 0)),
        ],
        out_specs=[
            pl.BlockSpec((tc, w), lambda b, h, c: (b * nt + c, h)),
            pl.BlockSpec((1, hps, DK, DK), lambda b, h, c: (b, h, 0, 0)),
        ],
        out_shape=[
            jax.ShapeDtypeStruct((m, heads * DK), BF16),
            jax.ShapeDtypeStruct(s0.shape, F32),
        ],
        scratch_shapes=[
            pltpu.VMEM((hps, DK, DK), F32),
            pltpu.VMEM((hps * n_chunks, 8 + CHUNK, DK), F32),
            pltpu.VMEM((hps * n_chunks, 8 + CHUNK, DK), F32),
        ],
        compiler_params=pltpu.CompilerParams(
            dimension_semantics=("arbitrary", "arbitrary", "arbitrary"), vmem_limit_bytes=VMEM_LIMIT),
        name="hgrn",
    )(z, z, z, z, lb_logits, gnorm.reshape(1, -1), s0, a_mat, lmask)


BIAS_SPAN = 3 * QBLK


BIAS_HEADS = 8


def _bias_kernel(rb_ref, o_ref):
    lane = lax.broadcasted_iota(jnp.int32, (1, BIAS_SPAN), 1)
    i = lax.broadcasted_iota(jnp.int32, (QBLK, QBLK), 0)
    j = lax.broadcasted_iota(jnp.int32, (QBLK, QBLK), 1)
    qc = i // CHUNK
    for h in range(BIAS_HEADS):
        rev = rb_ref[h:h + 1, :]
        far_val = rev[:, 0:1]
        row = jnp.where(lane < QBLK - 1, far_val, pltpu.roll(rev, QBLK - 1, axis=1))
        wide = jnp.broadcast_to(row, (QBLK, BIAS_SPAN))
        rolled = pltpu.roll(wide, QBLK + 1, axis=1, stride=1, stride_axis=0)
        near = (rolled[:, 0:QBLK], rolled[:, 2 * QBLK:3 * QBLK])
        far = jnp.broadcast_to(far_val, (QBLK, QBLK))
        for d in range(N_KBLK):
            kc = j // CHUNK - (QBLK // CHUNK) * d
            allowed = jnp.logical_and(kc <= qc, kc >= qc - BAND_CHUNKS)
            tile = near[d] if d < 2 else far
            o_ref[h, :, (N_KBLK - 1 - d) * QBLK:(N_KBLK - d) * QBLK] = jnp.where(allowed, tile * LOG2E, NEG_INF)


def _bias_table(rel_bias):
    heads = rel_bias.shape[0]
    assert rel_bias.shape[1] == N_REL and REL_CLIP == QBLK and heads % BIAS_HEADS == 0
    rev = jnp.pad(rel_bias[:, ::-1], ((0, 0), (0, BIAS_SPAN - N_REL)))
    return pl.pallas_call(
        _bias_kernel,
        grid=(heads // BIAS_HEADS,),
        in_specs=[pl.BlockSpec((BIAS_HEADS, BIAS_SPAN), lambda h: (h, 0))],
        out_specs=pl.BlockSpec((BIAS_HEADS, QBLK, N_KBLK * QBLK), lambda h: (h, 0, 0)),
        out_shape=jax.ShapeDtypeStruct((heads, QBLK, N_KBLK * QBLK), F32),
        compiler_params=pltpu.CompilerParams(dimension_semantics=("arbitrary",)),
        name="rel_bias_table",
    )(rev)


ROW_GROUP = 16


def _attn_scores(qm, k_ref, cols_k, bias_ref, hh, r_lo, s_ref, *, tq, clamp):
    if clamp:
        for t in range(N_KBLK):
            r = r_lo + QBLK * t
            cols = slice(QBLK * t, QBLK * (t + 1))
            kt = k_ref[pl.ds(pl.multiple_of(jnp.maximum(r, 0), QBLK), QBLK), cols_k]
            s = _dot_nt(qm, kt) + bias_ref[hh, 0:tq, cols]
            s_ref[0:tq, cols] = jnp.where(r >= 0, s, NEG_INF)
    else:
        rows = pl.ds(pl.multiple_of(r_lo, QBLK), N_KBLK * QBLK)
        s_ref[0:tq, :] = _dot_nt(qm, k_ref[rows, cols_k]) + bias_ref[hh, 0:tq, :]


def _attn_softmax(s_ref, p_ref, *, tq):
    sums = []
    for g in range(tq // ROW_GROUP):
        rows = slice(g * ROW_GROUP, (g + 1) * ROW_GROUP)
        s = s_ref[rows, :]
        p = jnp.exp2(s - jnp.max(s, axis=-1, keepdims=True))
        sums.append(jnp.sum(p, axis=-1, keepdims=True))
        p_ref[rows, :] = p.astype(BF16)
    return jnp.concatenate(sums, axis=0)


def _attn_values(p_ref, v_ref, cols_v, r_lo, *, tq, clamp):
    if clamp:
        parts = []
        for t in range(N_KBLK):
            r = pl.multiple_of(jnp.maximum(r_lo + QBLK * t, 0), QBLK)
            parts.append(_dot(p_ref[0:tq, QBLK * t:QBLK * (t + 1)], v_ref[pl.ds(r, QBLK), cols_v]))
        return functools.reduce(jnp.add, parts)
    rows = pl.ds(pl.multiple_of(r_lo, QBLK), N_KBLK * QBLK)
    return _dot(p_ref[0:tq, :], v_ref[rows, cols_v])


def _attn_kernel(q_ref, k_ref, v_ref, g_ref, bias_ref, o_ref, *scratch, tq, nqb, pps, koff):
    step = pl.program_id(2)
    n_chain = 2 * nqb * pps
    s_refs = scratch[:n_chain]
    p_refs = scratch[n_chain:]
    lane = lax.broadcasted_iota(jnp.int32, (tq, LANES), 1)
    first = lane < DH

    def compute(clamp):
        chains = [(qb, pp, hh) for qb in range(nqb) for pp in range(pps) for hh in range(2)]
        r_los = [(step * nqb + qb) * QBLK + koff - BAND_ROWS for qb in range(nqb)]
        sums, outs = {}, {}
        for step_c in range(n_chain + 2):
            if step_c < n_chain:
                qb, pp, hh = chains[step_c]
                cols = slice(pp * LANES, (pp + 1) * LANES)
                q2 = q_ref[qb * tq:(qb + 1) * tq, cols]
                qm = jnp.where(first if hh == 0 else jnp.logical_not(first), q2, jnp.zeros_like(q2))
                _attn_scores(qm, k_ref, cols, bias_ref, 2 * pp + hh, r_los[qb], s_refs[step_c],
                             tq=tq, clamp=clamp)
            if 0 <= step_c - 1 < n_chain:
                sums[step_c - 1] = _attn_softmax(s_refs[step_c - 1], p_refs[step_c - 1], tq=tq)
            if 0 <= step_c - 2 < n_chain:
                c = step_c - 2
                qb, pp, hh = chains[c]
                outs[c] = _attn_values(p_refs[c], v_ref, slice(pp * LANES, (pp + 1) * LANES), r_los[qb],
                                       tq=tq, clamp=clamp) / sums[c]
        for qb in range(nqb):
            for pp in range(pps):
                rows = slice(qb * tq, (qb + 1) * tq)
                cols = slice(pp * LANES, (pp + 1) * LANES)
                c0 = 2 * (qb * pps + pp)
                o = jnp.where(first, outs[c0], outs[c0 + 1])
                g = g_ref[rows, cols].astype(F32)
                o_ref[rows, cols] = (o * (g * _sigmoid(g))).astype(BF16)

    full_from = max(-(-(BAND_ROWS - koff) // (QBLK * nqb)), 0)
    if full_from == 0:
        compute(False)
    else:
        pl.when(step >= full_from)(lambda: compute(False))
        pl.when(step < full_from)(lambda: compute(True))


def _attention(q_arr, q_col, k_arr, k_col, v_arr, v_col, g_arr, g_col, bias, *,
               batch, q_rows, tq, nqb, pps, koff):
    k_rows = k_arr.shape[0] // batch
    ts = tq * nqb
    nq = q_rows // ts
    pairs = bias.shape[0] // 2
    assert q_rows % ts == 0 and (tq == QBLK or q_rows == tq) and pairs % pps == 0
    assert all(c % pps == 0 for c in (q_col, k_col, v_col, g_col))
    w = LANES * pps
    kern = functools.partial(_attn_kernel, tq=tq, nqb=nqb, pps=pps, koff=koff)
    return pl.pallas_call(
        kern,
        grid=(batch, pairs // pps, nq),
        in_specs=[
            pl.BlockSpec((ts, w), lambda b, p, i: (b * nq + i, q_col // pps + p)),
            pl.BlockSpec((k_rows, w), lambda b, p, i: (b, k_col // pps + p)),
            pl.BlockSpec((k_rows, w), lambda b, p, i: (b, v_col // pps + p)),
            pl.BlockSpec((ts, w), lambda b, p, i: (b * nq + i, g_col // pps + p)),
            pl.BlockSpec((2 * pps, QBLK, N_KBLK * QBLK), lambda b, p, i: (p, 0, 0)),
        ],
        out_specs=pl.BlockSpec((ts, w), lambda b, p, i: (b * nq + i, p)),
        out_shape=jax.ShapeDtypeStruct((batch * q_rows, pairs * LANES), BF16),
        scratch_shapes=([pltpu.VMEM((QBLK, N_KBLK * QBLK), F32)] * (2 * nqb * pps)
                        + [pltpu.VMEM((QBLK, N_KBLK * QBLK), BF16)] * (2 * nqb * pps)),
        compiler_params=pltpu.CompilerParams(
            dimension_semantics=("arbitrary", "arbitrary", "arbitrary"), vmem_limit_bytes=VMEM_LIMIT),
        name="band_attention",
    )(q_arr, k_arr, v_arr, g_arr, bias)


def _merge_out_kernel(oa_ref, ob_ref, ma_ref, mb_ref, x_ref, wpa_ref, wpb_ref, wo_ref, gp_ref, y_ref):
    pa = _dot(oa_ref[...], wpa_ref[...])
    pb = _dot(ob_ref[...], wpb_ref[...])
    merged = _sigmoid(ma_ref[...].astype(F32)) * pa + _sigmoid(mb_ref[...].astype(F32)) * pb
    hid = _dot(merged.astype(BF16), wo_ref[...])
    ms = jnp.mean(hid * hid, axis=-1, keepdims=True)
    y_ref[...] = x_ref[...] + (hid * lax.rsqrt(ms + EPS)) * gp_ref[...]


def _merge_out(o_a, o_b, z, x2, wpa_bf, wpb_bf, wo_bf, norm_post, *, tm=256):
    m, d = x2.shape
    d_a = o_a.shape[1]
    d_b = o_b.shape[1]
    ma_col = (4 * d_a + 4 * d_b) // d
    resident = dict(pipeline_mode=pl.Buffered(1))
    return pl.pallas_call(
        _merge_out_kernel,
        grid=(m // tm,),
        in_specs=[
            pl.BlockSpec((tm, d_a), lambda i: (i, 0)),
            pl.BlockSpec((tm, d_b), lambda i: (i, 0)),
            pl.BlockSpec((tm, d), lambda i: (i, ma_col)),
            pl.BlockSpec((tm, d), lambda i: (i, ma_col + 1)),
            pl.BlockSpec((tm, d), lambda i: (i, 0)),
            pl.BlockSpec((d_a, d), lambda i: (0, 0), **resident),
            pl.BlockSpec((d_b, d), lambda i: (0, 0), **resident),
            pl.BlockSpec((d, d), lambda i: (0, 0), **resident),
            pl.BlockSpec((1, d), lambda i: (0, 0)),
        ],
        out_specs=pl.BlockSpec((tm, d), lambda i: (i, 0)),
        out_shape=jax.ShapeDtypeStruct((m, d), F32),
        compiler_params=pltpu.CompilerParams(
            dimension_semantics=("arbitrary",), vmem_limit_bytes=VMEM_LIMIT),
        name="merge_out",
    )(o_a, o_b, z, z, x2, wpa_bf, wpb_bf, wo_bf, norm_post.reshape(1, d))


def kernel(x_prompt, x_sample, state_hgrn, cache_k, cache_v, norm_pre, w_in, lb_logits, gnorm_a,
           rel_bias, w_proj_a, w_proj_b, w_out, norm_post):
    depth = w_in.shape[0]
    assert depth == 1, "single-layer step"
    bp, tp, d = x_prompt.shape
    bs, ts, _ = x_sample.shape
    heads_a = state_hgrn.shape[2]
    heads_b = cache_k.shape[3]
    kv_rows = cache_k.shape[2]
    d_a = heads_a * DK
    d_b = heads_b * DH
    assert ts == CHUNK and kv_rows == BAND_ROWS and PAST_LEN % QBLK == 0 and tp % QBLK == 0

    w_in_bf = w_in[0].astype(BF16)
    wpa_bf = w_proj_a[0].astype(BF16)
    wpb_bf = w_proj_b[0].astype(BF16)
    wo_bf = w_out[0].astype(BF16)
    bias = _bias_table(rel_bias[0])
    col_b = 4 * d_a // LANES
    grp_b = d_b // LANES

    xp2 = x_prompt.reshape(bp * tp, d)
    rows_p = min(BAND_ROWS, tp)
    z_p, k_p, v_p = _in_proj(xp2, norm_pre[0], w_in_bf, rows_per_batch=tp, kv_rows=rows_p)
    oa_p, s_p = _hgrn(z_p, lb_logits, gnorm_a[0], jnp.zeros((bp, heads_a, DK, DK), F32),
                      batch=bp, rows_per_batch=tp, tc=2048, hps=1)
    ob_p = _attention(z_p, col_b, z_p, col_b + grp_b, z_p, col_b + 2 * grp_b, z_p, col_b + 3 * grp_b,
                      bias, batch=bp, q_rows=tp, tq=QBLK, nqb=8, pps=1, koff=0)
    y_p = _merge_out(oa_p, ob_p, z_p, xp2, wpa_bf, wpb_bf, wo_bf, norm_post[0])

    xs2 = x_sample.reshape(bs * ts, d)
    z_s, k_s, v_s = _in_proj(xs2, norm_pre[0], w_in_bf, rows_per_batch=ts, kv_rows=ts)
    oa_s, s_s = _hgrn(z_s, lb_logits, gnorm_a[0], state_hgrn[0],
                      batch=bs, rows_per_batch=ts, tc=ts, hps=heads_a)
    pad = jnp.zeros((bs, QBLK - ts, d_b), BF16)
    k_all = jnp.concatenate([cache_k[0].astype(BF16).reshape(bs, kv_rows, d_b),
                             k_s.reshape(bs, ts, d_b).astype(BF16), pad], axis=1)
    v_all = jnp.concatenate([cache_v[0].astype(BF16).reshape(bs, kv_rows, d_b),
                             v_s.reshape(bs, ts, d_b).astype(BF16), pad], axis=1)
    k_rows = kv_rows + QBLK
    ob_s = _attention(z_s, col_b, k_all.reshape(bs * k_rows, d_b), 0, v_all.reshape(bs * k_rows, d_b), 0,
                      z_s, col_b + 3 * grp_b, bias, batch=bs, q_rows=ts, tq=ts, nqb=1, pps=4, koff=kv_rows)
    y_s = _merge_out(oa_s, ob_s, z_s, xs2, wpa_bf, wpb_bf, wo_bf, norm_post[0])

    return (
        y_p.reshape(bp, tp, d),
        y_s.reshape(bs, ts, d),
        s_p[None],
        k_p.reshape(1, bp, rows_p, heads_b, DH),
        v_p.reshape(1, bp, rows_p, heads_b, DH),
        s_s[None],
        k_s.reshape(1, bs, ts, heads_b, DH),
        v_s.reshape(1, bs, ts, heads_b, DH),
    )
```

```python
import functools

import numpy as np
import jax
import jax.numpy as jnp
from jax import lax
from jax.experimental import pallas as pl
from jax.experimental.pallas import tpu as pltpu

F32 = jnp.float32
BF16 = jnp.bfloat16

LANES = 128
CHUNK = 64
DK = 128
DH = 64
BAND_CHUNKS = 8
BAND_ROWS = BAND_CHUNKS * CHUNK
REL_CLIP = 128
PAST_LEN = 2048
N_REL = 2 * REL_CLIP + 1
EPS = 1e-6
NEG_INF = -1e30
QBLK = 2 * CHUNK
N_KBLK = BAND_ROWS // QBLK + 1
VMEM_LIMIT = 56 * 1024 * 1024
LOG2E = 1.4426950408889634
Q_SCALE = DH ** -0.5 * LOG2E


def _dot(a, b):
    return jnp.dot(a, b, preferred_element_type=F32)


def _dot_nt(a, b):
    return lax.dot_general(a, b, (((1,), (1,)), ((), ())), preferred_element_type=F32)


def _dot_tn(a, b):
    return lax.dot_general(a, b, (((0,), (0,)), ((), ())), preferred_element_type=F32)


def _sigmoid(x):
    return 1.0 / (1.0 + jnp.exp(-x))


def _in_proj_kernel(x_ref, g_ref, cs_ref, w_ref, z_ref, k_ref, v_ref, xn_ref, *,
                    tiles_per_kv, kv_tail, k_at, v_at):
    i = pl.program_id(0)
    j = pl.program_id(1)
    tm = x_ref.shape[0]
    width = k_ref.shape[1]

    @pl.when(j == 0)
    def _():
        x = x_ref[...]
        ms = jnp.mean(x * x, axis=-1, keepdims=True)
        xn_ref[...] = ((x * lax.rsqrt(ms + EPS)) * g_ref[...]).astype(BF16)

    acc = _dot(xn_ref[...], w_ref[...])
    z_ref[...] = (acc * cs_ref[...]).astype(BF16)
    is_kv_tile = (i % tiles_per_kv) == (tiles_per_kv - 1)

    @pl.when(jnp.logical_and(is_kv_tile, j == k_at[0]))
    def _():
        k_ref[...] = acc[tm - kv_tail:, k_at[1]:k_at[1] + width]

    @pl.when(jnp.logical_and(is_kv_tile, j == v_at[0]))
    def _():
        v_ref[...] = acc[tm - kv_tail:, v_at[1]:v_at[1] + width]


def _in_proj(x2, norm_pre, w_in_bf, *, rows_per_batch, kv_rows, tm=1024, tn=1536):
    m, d = x2.shape
    d_in = w_in_bf.shape[1]
    d_b = (d_in - 2 * d) // 8
    assert m % tm == 0 and d_in % tn == 0
    if rows_per_batch >= tm:
        assert rows_per_batch % tm == 0 and kv_rows <= tm
        tiles_per_kv = rows_per_batch // tm
        kv_tail = kv_rows
    else:
        assert tm % rows_per_batch == 0 and kv_rows == rows_per_batch
        tiles_per_kv = 1
        kv_tail = tm
    n_kv = m // (tiles_per_kv * tm)
    k_at = divmod(5 * d_b, tn)
    v_at = divmod(6 * d_b, tn)
    assert k_at[1] + d_b <= tn and v_at[1] + d_b <= tn, "k / v column group must not straddle tiles"
    col = np.arange(d_in)
    col_scale = jnp.asarray(np.where((col >= 4 * d_b) & (col < 5 * d_b), Q_SCALE, 1.0)[None, :], F32)
    kern = functools.partial(_in_proj_kernel, tiles_per_kv=tiles_per_kv, kv_tail=kv_tail,
                             k_at=k_at, v_at=v_at)
    kv_spec = pl.BlockSpec((kv_tail, d_b), lambda i, j: (i // tiles_per_kv, 0))
    return pl.pallas_call(
        kern,
        grid=(m // tm, d_in // tn),
        in_specs=[
            pl.BlockSpec((tm, d), lambda i, j: (i, 0)),
            pl.BlockSpec((1, d), lambda i, j: (0, 0)),
            pl.BlockSpec((1, tn), lambda i, j: (0, j)),
            pl.BlockSpec((d, tn), lambda i, j: (0, j)),
        ],
        out_specs=[pl.BlockSpec((tm, tn), lambda i, j: (i, j)), kv_spec, kv_spec],
        out_shape=[
            jax.ShapeDtypeStruct((m, d_in), BF16),
            jax.ShapeDtypeStruct((n_kv * kv_tail, d_b), F32),
            jax.ShapeDtypeStruct((n_kv * kv_tail, d_b), F32),
        ],
        scratch_shapes=[pltpu.VMEM((tm, d), BF16)],
        compiler_params=pltpu.CompilerParams(
            dimension_semantics=("arbitrary", "arbitrary"), vmem_limit_bytes=VMEM_LIMIT),
        name="in_proj",
    )(x2, norm_pre.reshape(1, d), col_scale, w_in_bf)


HGRN_DIAG = 4
HGRN_LEVELS = tuple(m for m in (1, 2, 4, 8, 16, 32) if m >= HGRN_DIAG)


def _hgrn_sum_matrix():
    t = np.arange(CHUNK)[:, None]
    u = np.arange(CHUNK)[None, :]
    groups = [u <= t, u > t]
    for m in HGRN_LEVELS:
        ref = (t // (2 * m)) * 2 * m + m - 1
        groups.append(np.where(t > ref, (u > ref) & (u <= t), (u > t) & (u <= ref)))
    groups.append((u <= t) & (u // HGRN_DIAG == t // HGRN_DIAG))
    a = np.concatenate(groups, axis=0).astype(np.float32)
    return np.concatenate([a, a], axis=1)


def _hgrn_level_masks():
    t = np.arange(CHUNK)[:, None]
    s = np.arange(CHUNK)[None, :]
    masks = []
    for m in HGRN_LEVELS[:-1]:
        same = (t // (2 * m)) == (s // (2 * m))
        if m < 8:
            same = same & (t % (2 * m) >= m) & (s % (2 * m) < m)
        masks.append(same)
    return np.stack(masks).astype(np.float32)


def _hgrn_chunk(zq, zf, v, lb, oml, a2, st, level_masks, diag_masks, kpad_ref, cpad_ref):
    q = zq * _sigmoid(zq)
    sg = _sigmoid(zf)
    k = oml * (1.0 - sg)
    logf = jnp.log2(lb + oml * sg)

    hi = logf.astype(BF16)
    lo = (logf - hi.astype(F32)).astype(BF16)
    c = _dot(a2, jnp.concatenate([hi, lo], axis=0))
    b = c[0:CHUNK]
    b_rev = c[CHUNK:2 * CHUNK]
    c_loc = c[(2 + len(HGRN_LEVELS)) * CHUNK:]

    o_inter = _dot_nt((q * jnp.exp2(b)).astype(BF16), st.astype(BF16))
    kd = (k * jnp.exp2(b_rev)).astype(BF16)
    st_new = st * jnp.exp2(b[CHUNK - 1:CHUNK, :]) + _dot_tn(v, kd)

    zero8 = jnp.zeros((8, DK), F32)
    sc = None
    for lvl, m in enumerate(HGRN_LEVELS):
        zdec = jnp.exp2(c[(2 + lvl) * CHUNK:(3 + lvl) * CHUNK])
        if m < 8:
            q_s = q * zdec
            k_s = k * zdec
        else:
            q_rows, k_rows = [], []
            for g in range(CHUNK // 8):
                rows = slice(8 * g, 8 * g + 8)
                if (8 * g) % (2 * m) >= m:
                    q_rows.append(q[rows] * zdec[rows])
                    k_rows.append(zero8)
                else:
                    q_rows.append(zero8)
                    k_rows.append(k[rows] * zdec[rows])
            q_s = jnp.concatenate(q_rows, axis=0)
            k_s = jnp.concatenate(k_rows, axis=0)
        part = _dot_nt(q_s.astype(BF16), k_s.astype(BF16))
        if m in level_masks:
            part = part * level_masks[m]
        sc = part if sc is None else sc + part

    kpad_ref[8:, :] = k
    cpad_ref[8:, :] = c_loc
    diag = []
    for blk in range(CHUNK // 8):
        r0 = 8 * blk
        qb = q[r0:r0 + 8]
        cb = c_loc[r0:r0 + 8]
        acc = jnp.where(diag_masks[0], jnp.sum(qb * k[r0:r0 + 8], axis=-1, keepdims=True), 0.0)
        for dl in range(1, HGRN_DIAG):
            ks = kpad_ref[8 + r0 - dl:16 + r0 - dl, :]
            cs = cpad_ref[8 + r0 - dl:16 + r0 - dl, :]
            dec = jnp.exp2(jnp.minimum(cb - cs, 0.0))
            col = jnp.sum((qb * ks) * dec, axis=-1, keepdims=True)
            acc = jnp.where(diag_masks[dl], col, acc)
        diag.append(pltpu.roll(acc, r0, axis=1) if blk else acc)
    sc = sc + jnp.concatenate(diag, axis=0)[:, :CHUNK]

    o = o_inter + _dot(sc.astype(BF16), v)
    return o, st_new


def _hgrn_kernel(zq_ref, zf_ref, zi_ref, zg_ref, lbl_ref, gn_ref, s0_ref, a_ref, lm_ref,
                 o_ref, sfin_ref, st_ref, kpad_ref, cpad_ref, *, n_chunks, hps):
    c_idx = pl.program_id(2)

    @pl.when(c_idx == 0)
    def _():
        for hh in range(hps):
            st_ref[hh] = s0_ref[0, hh].T

    a2 = a_ref[...]
    level_masks = {m: lm_ref[i] for i, m in enumerate(HGRN_LEVELS[:-1])}
    sub = lax.broadcasted_iota(jnp.int32, (8, DK), 0)
    lane = lax.broadcasted_iota(jnp.int32, (8, DK), 1)
    diag_masks = [jnp.logical_and(lane == sub - dl, sub % HGRN_DIAG >= dl) for dl in range(HGRN_DIAG)]

    kpad_ref[:, 0:8, :] = jnp.zeros((hps * n_chunks, 8, DK), F32)
    cpad_ref[:, 0:8, :] = jnp.zeros((hps * n_chunks, 8, DK), F32)

    lbl = lbl_ref[...]
    e = jnp.exp(lbl - jnp.max(lbl, axis=0, keepdims=True))
    lb_all = e[0:1] / jnp.sum(e, axis=0, keepdims=True)

    state = [st_ref[hh] for hh in range(hps)]
    for slot, (hh, ci) in enumerate([(hh, ci) for hh in range(hps) for ci in range(n_chunks)]):
        rows, cols = slice(ci * CHUNK, (ci + 1) * CHUNK), slice(hh * DK, (hh + 1) * DK)
        lb = lb_all[:, cols]
        zg = zg_ref[rows, cols].astype(F32)
        o, state[hh] = _hgrn_chunk(zq_ref[rows, cols].astype(F32), zf_ref[rows, cols].astype(F32),
                                   zi_ref[rows, cols], lb, 1.0 - lb, a2, state[hh], level_masks,
                                   diag_masks, kpad_ref.at[slot], cpad_ref.at[slot])
        o = o * lax.rsqrt(jnp.mean(o * o, axis=-1, keepdims=True) + EPS)
        o_ref[rows, cols] = ((o * gn_ref[:, cols]) * (zg * _sigmoid(zg))).astype(BF16)
    for hh in range(hps):
        st_ref[hh] = state[hh]

    @pl.when(c_idx == pl.num_programs(2) - 1)
    def _():
        for hh in range(hps):
            sfin_ref[0, hh] = st_ref[hh].T


def _hgrn(z, lb_logits, gnorm, s0, *, batch, rows_per_batch, tc, hps):
    m = z.shape[0]
    heads = s0.shape[1]
    nt = rows_per_batch // tc
    n_chunks = tc // CHUNK
    hg = heads // hps
    assert heads % hps == 0
    a_mat = jnp.asarray(_hgrn_sum_matrix(), BF16)
    lmask = jnp.asarray(_hgrn_level_masks(), F32)
    w = DK * hps

    def zspec(group):
        return pl.BlockSpec((tc, w), lambda b, h, c: (b * nt + c, group * hg + h))

    kern = functools.partial(_hgrn_kernel, n_chunks=n_chunks, hps=hps)
    return pl.pallas_call(
        kern,
        grid=(batch, hg, nt),
        in_specs=[
            zspec(0), zspec(1), zspec(2), zspec(3),
            pl.BlockSpec((lb_logits.shape[0], w), lambda b, h, c: (0, h)),
            pl.BlockSpec((1, w), lambda b, h, c: (0, h)),
            pl.BlockSpec((1, hps, DK, DK), lambda b, h, c: (b, h, 0, 0)),
            pl.BlockSpec(a_mat.shape, lambda b, h, c: (0, 0)),
            pl.BlockSpec(lmask.shape, lambda b, h, c: (0, 0, 0)),
        ],
        out_specs=[
            pl.BlockSpec((tc, w), lambda b, h, c: (b * nt + c, h)),
            pl.BlockSpec((1, hps, DK, DK), lambda b, h, c: (b, h, 0, 0)),
        ],
        out_shape=[
            jax.ShapeDtypeStruct((m, heads * DK), BF16),
            jax.ShapeDtypeStruct(s0.shape, F32),
        ],
        scratch_shapes=[
            pltpu.VMEM((hps, DK, DK), F32),
            pltpu.VMEM((hps * n_chunks, 8 + CHUNK, DK), F32),
            pltpu.VMEM((hps * n_chunks, 8 + CHUNK, DK), F32),
        ],
        compiler_params=pltpu.CompilerParams(
            dimension_semantics=("arbitrary", "arbitrary", "arbitrary"), vmem_limit_bytes=VMEM_LIMIT),
        name="hgrn",
    )(z, z, z, z, lb_logits, gnorm.reshape(1, -1), s0, a_mat, lmask)


BIAS_SPAN = 3 * QBLK


BIAS_HEADS = 8


def _bias_kernel(rb_ref, o_ref):
    lane = lax.broadcasted_iota(jnp.int32, (1, BIAS_SPAN), 1)
    i = lax.broadcasted_iota(jnp.int32, (QBLK, QBLK), 0)
    j = lax.broadcasted_iota(jnp.int32, (QBLK, QBLK), 1)
    qc = i // CHUNK
    for h in range(BIAS_HEADS):
        rev = rb_ref[h:h + 1, :]
        far_val = rev[:, 0:1]
        row = jnp.where(lane < QBLK - 1, far_val, pltpu.roll(rev, QBLK - 1, axis=1))
        wide = jnp.broadcast_to(row, (QBLK, BIAS_SPAN))
        rolled = pltpu.roll(wide, QBLK + 1, axis=1, stride=1, stride_axis=0)
        near = (rolled[:, 0:QBLK], rolled[:, 2 * QBLK:3 * QBLK])
        far = jnp.broadcast_to(far_val, (QBLK, QBLK))
        for d in range(N_KBLK):
            kc = j // CHUNK - (QBLK // CHUNK) * d
            allowed = jnp.logical_and(kc <= qc, kc >= qc - BAND_CHUNKS)
            tile = near[d] if d < 2 else far
            o_ref[h, :, (N_KBLK - 1 - d) * QBLK:(N_KBLK - d) * QBLK] = jnp.where(allowed, tile * LOG2E, NEG_INF)


def _bias_table(rel_bias):
    heads = rel_bias.shape[0]
    assert rel_bias.shape[1] == N_REL and REL_CLIP == QBLK and heads % BIAS_HEADS == 0
    rev = jnp.pad(rel_bias[:, ::-1], ((0, 0), (0, BIAS_SPAN - N_REL)))
    return pl.pallas_call(
        _bias_kernel,
        grid=(heads // BIAS_HEADS,),
        in_specs=[pl.BlockSpec((BIAS_HEADS, BIAS_SPAN), lambda h: (h, 0))],
        out_specs=pl.BlockSpec((BIAS_HEADS, QBLK, N_KBLK * QBLK), lambda h: (h, 0, 0)),
        out_shape=jax.ShapeDtypeStruct((heads, QBLK, N_KBLK * QBLK), F32),
        compiler_params=pltpu.CompilerParams(dimension_semantics=("arbitrary",)),
        name="rel_bias_table",
    )(rev)


ROW_GROUP = 16


def _attn_scores(qm, k_ref, cols_k, bias_ref, hh, r_lo, s_ref, *, tq, clamp):
    if clamp:
        for t in range(N_KBLK):
            r = r_lo + QBLK * t
            cols = slice(QBLK * t, QBLK * (t + 1))
            kt = k_ref[pl.ds(pl.multiple_of(jnp.maximum(r, 0), QBLK), QBLK), cols_k]
            s = _dot_nt(qm, kt) + bias_ref[hh, 0:tq, cols]
            s_ref[0:tq, cols] = jnp.where(r >= 0, s, NEG_INF)
    else:
        rows = pl.ds(pl.multiple_of(r_lo, QBLK), N_KBLK * QBLK)
        s_ref[0:tq, :] = _dot_nt(qm, k_ref[rows, cols_k]) + bias_ref[hh, 0:tq, :]


def _attn_softmax(s_ref, p_ref, *, tq):
    sums = []
    for g in range(tq // ROW_GROUP):
        rows = slice(g * ROW_GROUP, (g + 1) * ROW_GROUP)
        s = s_ref[rows, :]
        p = jnp.exp2(s - jnp.max(s, axis=-1, keepdims=True))
        sums.append(jnp.sum(p, axis=-1, keepdims=True))
        p_ref[rows, :] = p.astype(BF16)
    return jnp.concatenate(sums, axis=0)


def _attn_values(p_ref, v_ref, cols_v, r_lo, *, tq, clamp):
    if clamp:
        parts = []
        for t in range(N_KBLK):
            r = pl.multiple_of(jnp.maximum(r_lo + QBLK * t, 0), QBLK)
            parts.append(_dot(p_ref[0:tq, QBLK * t:QBLK * (t + 1)], v_ref[pl.ds(r, QBLK), cols_v]))
        return functools.reduce(jnp.add, parts)
    rows = pl.ds(pl.multiple_of(r_lo, QBLK), N_KBLK * QBLK)
    return _dot(p_ref[0:tq, :], v_ref[rows, cols_v])


def _attn_kernel(q_ref, k_ref, v_ref, g_ref, bias_ref, o_ref, *scratch, tq, nqb, pps, koff):
    step = pl.program_id(2)
    n_chain = 2 * nqb * pps
    s_refs = scratch[:n_chain]
    p_refs = scratch[n_chain:]
    lane = lax.broadcasted_iota(jnp.int32, (tq, LANES), 1)
    first = lane < DH

    def compute(clamp):
        chains = [(qb, pp, hh) for qb in range(nqb) for pp in range(pps) for hh in range(2)]
        r_los = [(step * nqb + qb) * QBLK + koff - BAND_ROWS for qb in range(nqb)]
        sums, outs = {}, {}
        for step_c in range(n_chain + 2):
            if step_c < n_chain:
                qb, pp, hh = chains[step_c]
                cols = slice(pp * LANES, (pp + 1) * LANES)
                q2 = q_ref[qb * tq:(qb + 1) * tq, cols]
                qm = jnp.where(first if hh == 0 else jnp.logical_not(first), q2, jnp.zeros_like(q2))
                _attn_scores(qm, k_ref, cols, bias_ref, 2 * pp + hh, r_los[qb], s_refs[step_c],
                             tq=tq, clamp=clamp)
            if 0 <= step_c - 1 < n_chain:
                sums[step_c - 1] = _attn_softmax(s_refs[step_c - 1], p_refs[step_c - 1], tq=tq)
            if 0 <= step_c - 2 < n_chain:
                c = step_c - 2
                qb, pp, hh = chains[c]
                outs[c] = _attn_values(p_refs[c], v_ref, slice(pp * LANES, (pp + 1) * LANES), r_los[qb],
                                       tq=tq, clamp=clamp) / sums[c]
        for qb in range(nqb):
            for pp in range(pps):
                rows = slice(qb * tq, (qb + 1) * tq)
                cols = slice(pp * LANES, (pp + 1) * LANES)
                c0 = 2 * (qb * pps + pp)
                o = jnp.where(first, outs[c0], outs[c0 + 1])
                g = g_ref[rows, cols].astype(F32)
                o_ref[rows, cols] = (o * (g * _sigmoid(g))).astype(BF16)

    full_from = max(-(-(BAND_ROWS - koff) // (QBLK * nqb)), 0)
    if full_from == 0:
        compute(False)
    else:
        pl.when(step >= full_from)(lambda: compute(False))
        pl.when(step < full_from)(lambda: compute(True))


def _attention(q_arr, q_col, k_arr, k_col, v_arr, v_col, g_arr, g_col, bias, *,
               batch, q_rows, tq, nqb, pps, koff):
    k_rows = k_arr.shape[0] // batch
    ts = tq * nqb
    nq = q_rows // ts
    pairs = bias.shape[0] // 2
    assert q_rows % ts == 0 and (tq == QBLK or q_rows == tq) and pairs % pps == 0
    assert all(c % pps == 0 for c in (q_col, k_col, v_col, g_col))
    w = LANES * pps
    kern = functools.partial(_attn_kernel, tq=tq, nqb=nqb, pps=pps, koff=koff)
    return pl.pallas_call(
        kern,
        grid=(pairs // pps, batch, nq),
        in_specs=[
            pl.BlockSpec((ts, w), lambda p, b, i: (b * nq + i, q_col // pps + p)),
            pl.BlockSpec((k_rows, w), lambda p, b, i: (b, k_col // pps + p)),
            pl.BlockSpec((k_rows, w), lambda p, b, i: (b, v_col // pps + p)),
            pl.BlockSpec((ts, w), lambda p, b, i: (b * nq + i, g_col // pps + p)),
            pl.BlockSpec((2 * pps, QBLK, N_KBLK * QBLK), lambda p, b, i: (p, 0, 0)),
        ],
        out_specs=pl.BlockSpec((ts, w), lambda p, b, i: (b * nq + i, p)),
        out_shape=jax.ShapeDtypeStruct((batch * q_rows, pairs * LANES), BF16),
        scratch_shapes=([pltpu.VMEM((QBLK, N_KBLK * QBLK), F32)] * (2 * nqb * pps)
                        + [pltpu.VMEM((QBLK, N_KBLK * QBLK), BF16)] * (2 * nqb * pps)),
        compiler_params=pltpu.CompilerParams(
            dimension_semantics=("arbitrary", "arbitrary", "arbitrary"), vmem_limit_bytes=VMEM_LIMIT),
        name="band_attention",
    )(q_arr, k_arr, v_arr, g_arr, bias)


def _merge_out_kernel(oa_ref, ob_ref, ma_ref, mb_ref, x_ref, wpa_ref, wpb_ref, wo_ref, gp_ref, y_ref):
    pa = _dot(oa_ref[...], wpa_ref[...])
    pb = _dot(ob_ref[...], wpb_ref[...])
    merged = _sigmoid(ma_ref[...].astype(F32)) * pa + _sigmoid(mb_ref[...].astype(F32)) * pb
    hid = _dot(merged.astype(BF16), wo_ref[...])
    ms = jnp.mean(hid * hid, axis=-1, keepdims=True)
    y_ref[...] = x_ref[...] + (hid * lax.rsqrt(ms + EPS)) * gp_ref[...]


def _merge_out(o_a, o_b, z, x2, wpa_bf, wpb_bf, wo_bf, norm_post, *, tm=256):
    m, d = x2.shape
    d_a = o_a.shape[1]
    d_b = o_b.shape[1]
    ma_col = (4 * d_a + 4 * d_b) // d
    resident = dict(pipeline_mode=pl.Buffered(1))
    return pl.pallas_call(
        _merge_out_kernel,
        grid=(m // tm,),
        in_specs=[
            pl.BlockSpec((tm, d_a), lambda i: (i, 0)),
            pl.BlockSpec((tm, d_b), lambda i: (i, 0)),
            pl.BlockSpec((tm, d), lambda i: (i, ma_col)),
            pl.BlockSpec((tm, d), lambda i: (i, ma_col + 1)),
            pl.BlockSpec((tm, d), lambda i: (i, 0)),
            pl.BlockSpec((d_a, d), lambda i: (0, 0), **resident),
            pl.BlockSpec((d_b, d), lambda i: (0, 0), **resident),
            pl.BlockSpec((d, d), lambda i: (0, 0), **resident),
            pl.BlockSpec((1, d), lambda i: (0, 0)),
        ],
        out_specs=pl.BlockSpec((tm, d), lambda i: (i, 0)),
        out_shape=jax.ShapeDtypeStruct((m, d), F32),
        compiler_params=pltpu.CompilerParams(
            dimension_semantics=("arbitrary",), vmem_limit_bytes=VMEM_LIMIT),
        name="merge_out",
    )(o_a, o_b, z, z, x2, wpa_bf, wpb_bf, wo_bf, norm_post.reshape(1, d))


def kernel(x_prompt, x_sample, state_hgrn, cache_k, cache_v, norm_pre, w_in, lb_logits, gnorm_a,
           rel_bias, w_proj_a, w_proj_b, w_out, norm_post):
    depth = w_in.shape[0]
    assert depth == 1, "single-layer step"
    bp, tp, d = x_prompt.shape
    bs, ts, _ = x_sample.shape
    heads_a = state_hgrn.shape[2]
    heads_b = cache_k.shape[3]
    kv_rows = cache_k.shape[2]
    d_a = heads_a * DK
    d_b = heads_b * DH
    assert ts == CHUNK and kv_rows == BAND_ROWS and PAST_LEN % QBLK == 0 and tp % QBLK == 0

    w_in_bf = w_in[0].astype(BF16)
    wpa_bf = w_proj_a[0].astype(BF16)
    wpb_bf = w_proj_b[0].astype(BF16)
    wo_bf = w_out[0].astype(BF16)
    bias = _bias_table(rel_bias[0])
    col_b = 4 * d_a // LANES
    grp_b = d_b // LANES

    xp2 = x_prompt.reshape(bp * tp, d)
    rows_p = min(BAND_ROWS, tp)
    z_p, k_p, v_p = _in_proj(xp2, norm_pre[0], w_in_bf, rows_per_batch=tp, kv_rows=rows_p)
    oa_p, s_p = _hgrn(z_p, lb_logits, gnorm_a[0], jnp.zeros((bp, heads_a, DK, DK), F32),
                      batch=bp, rows_per_batch=tp, tc=4096, hps=1)
    ob_p = _attention(z_p, col_b, z_p, col_b + grp_b, z_p, col_b + 2 * grp_b, z_p, col_b + 3 * grp_b,
                      bias, batch=bp, q_rows=tp, tq=QBLK, nqb=8, pps=1, koff=0)
    y_p = _merge_out(oa_p, ob_p, z_p, xp2, wpa_bf, wpb_bf, wo_bf, norm_post[0])

    xs2 = x_sample.reshape(bs * ts, d)
    z_s, k_s, v_s = _in_proj(xs2, norm_pre[0], w_in_bf, rows_per_batch=ts, kv_rows=ts)
    oa_s, s_s = _hgrn(z_s, lb_logits, gnorm_a[0], state_hgrn[0],
                      batch=bs, rows_per_batch=ts, tc=ts, hps=heads_a)
    pad = jnp.zeros((bs, QBLK - ts, d_b), BF16)
    k_all = jnp.concatenate([cache_k[0].astype(BF16).reshape(bs, kv_rows, d_b),
                             k_s.reshape(bs, ts, d_b).astype(BF16), pad], axis=1)
    v_all = jnp.concatenate([cache_v[0].astype(BF16).reshape(bs, kv_rows, d_b),
                             v_s.reshape(bs, ts, d_b).astype(BF16), pad], axis=1)
    k_rows = kv_rows + QBLK
    ob_s = _attention(z_s, col_b, k_all.reshape(bs * k_rows, d_b), 0, v_all.reshape(bs * k_rows, d_b), 0,
                      z_s, col_b + 3 * grp_b, bias, batch=bs, q_rows=ts, tq=ts, nqb=1, pps=4, koff=kv_rows)
    y_s = _merge_out(oa_s, ob_s, z_s, xs2, wpa_bf, wpb_bf, wo_bf, norm_post[0])

    return (
        y_p.reshape(bp, tp, d),
        y_s.reshape(bs, ts, d),
        s_p[None],
        k_p.reshape(1, bp, rows_p, heads_b, DH),
        v_p.reshape(1, bp, rows_p, heads_b, DH),
        s_s[None],
        k_s.reshape(1, bs, ts, heads_b, DH),
        v_s.reshape(1, bs, ts, heads_b, DH),
    )
```

```python
import functools

import numpy as np
import jax
import jax.numpy as jnp
from jax import lax
from jax.experimental import pallas as pl
from jax.experimental.pallas import tpu as pltpu

F32 = jnp.float32
BF16 = jnp.bfloat16

LANES = 128
CHUNK = 64
DK = 128
DH = 64
BAND_CHUNKS = 8
BAND_ROWS = BAND_CHUNKS * CHUNK
REL_CLIP = 128
PAST_LEN = 2048
N_REL = 2 * REL_CLIP + 1
EPS = 1e-6
NEG_INF = -1e30
QBLK = 2 * CHUNK
N_KBLK = BAND_ROWS // QBLK + 1
VMEM_LIMIT = 56 * 1024 * 1024
LOG2E = 1.4426950408889634
Q_SCALE = DH ** -0.5 * LOG2E


def _dot(a, b):
    return jnp.dot(a, b, preferred_element_type=F32)


def _dot_nt(a, b):
    return lax.dot_general(a, b, (((1,), (1,)), ((), ())), preferred_element_type=F32)


def _dot_tn(a, b):
    return lax.dot_general(a, b, (((0,), (0,)), ((), ())), preferred_element_type=F32)


def _sigmoid(x):
    return 1.0 / (1.0 + jnp.exp(-x))


def _in_proj_kernel(x_ref, g_ref, cs_ref, w_ref, z_ref, k_ref, v_ref, xn_ref, *,
                    tiles_per_kv, kv_tail, k_at, v_at):
    i = pl.program_id(0)
    j = pl.program_id(1)
    tm = x_ref.shape[0]
    width = k_ref.shape[1]

    @pl.when(j == 0)
    def _():
        x = x_ref[...]
        ms = jnp.mean(x * x, axis=-1, keepdims=True)
        xn_ref[...] = ((x * lax.rsqrt(ms + EPS)) * g_ref[...]).astype(BF16)

    acc = _dot(xn_ref[...], w_ref[...])
    z_ref[...] = (acc * cs_ref[...]).astype(BF16)
    is_kv_tile = (i % tiles_per_kv) == (tiles_per_kv - 1)

    @pl.when(jnp.logical_and(is_kv_tile, j == k_at[0]))
    def _():
        k_ref[...] = acc[tm - kv_tail:, k_at[1]:k_at[1] + width]

    @pl.when(jnp.logical_and(is_kv_tile, j == v_at[0]))
    def _():
        v_ref[...] = acc[tm - kv_tail:, v_at[1]:v_at[1] + width]


def _in_proj(x2, norm_pre, w_in_bf, *, rows_per_batch, kv_rows, tm=1024, tn=1536):
    m, d = x2.shape
    d_in = w_in_bf.shape[1]
    d_b = (d_in - 2 * d) // 8
    assert m % tm == 0 and d_in % tn == 0
    if rows_per_batch >= tm:
        assert rows_per_batch % tm == 0 and kv_rows <= tm
        tiles_per_kv = rows_per_batch // tm
        kv_tail = kv_rows
    else:
        assert tm % rows_per_batch == 0 and kv_rows == rows_per_batch
        tiles_per_kv = 1
        kv_tail = tm
    n_kv = m // (tiles_per_kv * tm)
    k_at = divmod(5 * d_b, tn)
    v_at = divmod(6 * d_b, tn)
    assert k_at[1] + d_b <= tn and v_at[1] + d_b <= tn, "k / v column group must not straddle tiles"
    col = np.arange(d_in)
    col_scale = jnp.asarray(np.where((col >= 4 * d_b) & (col < 5 * d_b), Q_SCALE, 1.0)[None, :], F32)
    kern = functools.partial(_in_proj_kernel, tiles_per_kv=tiles_per_kv, kv_tail=kv_tail,
                             k_at=k_at, v_at=v_at)
    kv_spec = pl.BlockSpec((kv_tail, d_b), lambda i, j: (i // tiles_per_kv, 0))
    return pl.pallas_call(
        kern,
        grid=(m // tm, d_in // tn),
        in_specs=[
            pl.BlockSpec((tm, d), lambda i, j: (i, 0)),
            pl.BlockSpec((1, d), lambda i, j: (0, 0)),
            pl.BlockSpec((1, tn), lambda i, j: (0, j)),
            pl.BlockSpec((d, tn), lambda i, j: (0, j)),
        ],
        out_specs=[pl.BlockSpec((tm, tn), lambda i, j: (i, j)), kv_spec, kv_spec],
        out_shape=[
            jax.ShapeDtypeStruct((m, d_in), BF16),
            jax.ShapeDtypeStruct((n_kv * kv_tail, d_b), F32),
            jax.ShapeDtypeStruct((n_kv * kv_tail, d_b), F32),
        ],
        scratch_shapes=[pltpu.VMEM((tm, d), BF16)],
        compiler_params=pltpu.CompilerParams(
            dimension_semantics=("arbitrary", "arbitrary"), vmem_limit_bytes=VMEM_LIMIT),
        name="in_proj",
    )(x2, norm_pre.reshape(1, d), col_scale, w_in_bf)


HGRN_DIAG = 4
HGRN_LEVELS = tuple(m for m in (1, 2, 4, 8, 16, 32) if m >= HGRN_DIAG)


def _hgrn_sum_matrix():
    t = np.arange(CHUNK)[:, None]
    u = np.arange(CHUNK)[None, :]
    groups = [u <= t, u > t]
    for m in HGRN_LEVELS:
        ref = (t // (2 * m)) * 2 * m + m - 1
        groups.append(np.where(t > ref, (u > ref) & (u <= t), (u > t) & (u <= ref)))
    groups.append((u <= t) & (u // HGRN_DIAG == t // HGRN_DIAG))
    a = np.concatenate(groups, axis=0).astype(np.float32)
    return np.concatenate([a, a], axis=1)


def _hgrn_level_masks():
    t = np.arange(CHUNK)[:, None]
    s = np.arange(CHUNK)[None, :]
    masks = []
    for m in HGRN_LEVELS[:-1]:
        same = (t // (2 * m)) == (s // (2 * m))
        if m < 8:
            same = same & (t % (2 * m) >= m) & (s % (2 * m) < m)
        masks.append(same)
    return np.stack(masks).astype(np.float32)


def _hgrn_chunk(zq, zf, v, lb, oml, a2, st, level_masks, diag_masks, kpad_ref, cpad_ref):
    q = zq * _sigmoid(zq)
    sg = _sigmoid(zf)
    k = oml * (1.0 - sg)
    logf = jnp.log2(lb + oml * sg)

    hi = logf.astype(BF16)
    lo = (logf - hi.astype(F32)).astype(BF16)
    c = _dot(a2, jnp.concatenate([hi, lo], axis=0))
    b = c[0:CHUNK]
    b_rev = c[CHUNK:2 * CHUNK]
    c_loc = c[(2 + len(HGRN_LEVELS)) * CHUNK:]

    o_inter = _dot_nt((q * jnp.exp2(b)).astype(BF16), st.astype(BF16))
    kd = (k * jnp.exp2(b_rev)).astype(BF16)
    st_new = st * jnp.exp2(b[CHUNK - 1:CHUNK, :]) + _dot_tn(v, kd)

    zero8 = jnp.zeros((8, DK), F32)
    sc = None
    for lvl, m in enumerate(HGRN_LEVELS):
        zdec = jnp.exp2(c[(2 + lvl) * CHUNK:(3 + lvl) * CHUNK])
        if m < 8:
            q_s = q * zdec
            k_s = k * zdec
        else:
            q_rows, k_rows = [], []
            for g in range(CHUNK // 8):
                rows = slice(8 * g, 8 * g + 8)
                if (8 * g) % (2 * m) >= m:
                    q_rows.append(q[rows] * zdec[rows])
                    k_rows.append(zero8)
                else:
                    q_rows.append(zero8)
                    k_rows.append(k[rows] * zdec[rows])
            q_s = jnp.concatenate(q_rows, axis=0)
            k_s = jnp.concatenate(k_rows, axis=0)
        part = _dot_nt(q_s.astype(BF16), k_s.astype(BF16))
        if m in level_masks:
            part = part * level_masks[m]
        sc = part if sc is None else sc + part

    kpad_ref[8:, :] = k
    cpad_ref[8:, :] = c_loc
    diag = []
    for blk in range(CHUNK // 8):
        r0 = 8 * blk
        qb = q[r0:r0 + 8]
        cb = c_loc[r0:r0 + 8]
        acc = jnp.where(diag_masks[0], jnp.sum(qb * k[r0:r0 + 8], axis=-1, keepdims=True), 0.0)
        for dl in range(1, HGRN_DIAG):
            ks = kpad_ref[8 + r0 - dl:16 + r0 - dl, :]
            cs = cpad_ref[8 + r0 - dl:16 + r0 - dl, :]
            dec = jnp.exp2(jnp.minimum(cb - cs, 0.0))
            col = jnp.sum((qb * ks) * dec, axis=-1, keepdims=True)
            acc = jnp.where(diag_masks[dl], col, acc)
        diag.append(pltpu.roll(acc, r0, axis=1) if blk else acc)
    sc = sc + jnp.concatenate(diag, axis=0)[:, :CHUNK]

    o = o_inter + _dot(sc.astype(BF16), v)
    return o, st_new


def _hgrn_kernel(zq_ref, zf_ref, zi_ref, zg_ref, lbl_ref, gn_ref, s0_ref, a_ref, lm_ref,
                 o_ref, sfin_ref, st_ref, kpad_ref, cpad_ref, *, n_chunks, hps):
    c_idx = pl.program_id(2)

    @pl.when(c_idx == 0)
    def _():
        for hh in range(hps):
            st_ref[hh] = s0_ref[0, hh].T

    a2 = a_ref[...]
    level_masks = {m: lm_ref[i] for i, m in enumerate(HGRN_LEVELS[:-1])}
    sub = lax.broadcasted_iota(jnp.int32, (8, DK), 0)
    lane = lax.broadcasted_iota(jnp.int32, (8, DK), 1)
    diag_masks = [jnp.logical_and(lane == sub - dl, sub % HGRN_DIAG >= dl) for dl in range(HGRN_DIAG)]

    kpad_ref[:, 0:8, :] = jnp.zeros((hps * n_chunks, 8, DK), F32)
    cpad_ref[:, 0:8, :] = jnp.zeros((hps * n_chunks, 8, DK), F32)

    lbl = lbl_ref[...]
    e = jnp.exp(lbl - jnp.max(lbl, axis=0, keepdims=True))
    lb_all = e[0:1] / jnp.sum(e, axis=0, keepdims=True)

    state = [st_ref[hh] for hh in range(hps)]
    for slot, (hh, ci) in enumerate([(hh, ci) for hh in range(hps) for ci in range(n_chunks)]):
        rows, cols = slice(ci * CHUNK, (ci + 1) * CHUNK), slice(hh * DK, (hh + 1) * DK)
        lb = lb_all[:, cols]
        zg = zg_ref[rows, cols].astype(F32)
        o, state[hh] = _hgrn_chunk(zq_ref[rows, cols].astype(F32), zf_ref[rows, cols].astype(F32),
                                   zi_ref[rows, cols], lb, 1.0 - lb, a2, state[hh], level_masks,
                                   diag_masks, kpad_ref.at[slot], cpad_ref.at[slot])
        o = o * lax.rsqrt(jnp.mean(o * o, axis=-1, keepdims=True) + EPS)
        o_ref[rows, cols] = ((o * gn_ref[:, cols]) * (zg * _sigmoid(zg))).astype(BF16)
    for hh in range(hps):
        st_ref[hh] = state[hh]

    @pl.when(c_idx == pl.num_programs(2) - 1)
    def _():
        for hh in range(hps):
            sfin_ref[0, hh] = st_ref[hh].T


def _hgrn(z, lb_logits, gnorm, s0, *, batch, rows_per_batch, tc, hps):
    m = z.shape[0]
    heads = s0.shape[1]
    nt = rows_per_batch // tc
    n_chunks = tc // CHUNK
    hg = heads // hps
    assert heads % hps == 0
    a_mat = jnp.asarray(_hgrn_sum_matrix(), BF16)
    lmask = jnp.asarray(_hgrn_level_masks(), F32)
    w = DK * hps

    def zspec(group):
        return pl.BlockSpec((tc, w), lambda b, h, c: (b * nt + c, group * hg + h))

    kern = functools.partial(_hgrn_kernel, n_chunks=n_chunks, hps=hps)
    return pl.pallas_call(
        kern,
        grid=(batch, hg, nt),
        in_specs=[
            zspec(0), zspec(1), zspec(2), zspec(3),
            pl.BlockSpec((lb_logits.shape[0], w), lambda b, h, c: (0, h)),
            pl.BlockSpec((1, w), lambda b, h, c: (0, h)),
            pl.BlockSpec((1, hps, DK, DK), lambda b, h, c: (b, h, 0, 0)),
            pl.BlockSpec(a_mat.shape, lambda b, h, c: (0, 0)),
            pl.BlockSpec(lmask.shape, lambda b, h, c: (0, 0, 0)),
        ],
        out_specs=[
            pl.BlockSpec((tc, w), lambda b, h, c: (b * nt + c, h)),
            pl.BlockSpec((1, hps, DK, DK), lambda b, h, c: (b, h, 0, 0)),
        ],
        out_shape=[
            jax.ShapeDtypeStruct((m, heads * DK), BF16),
            jax.ShapeDtypeStruct(s0.shape, F32),
        ],
        scratch_shapes=[
            pltpu.VMEM((hps, DK, DK), F32),
            pltpu.VMEM((hps * n_chunks, 8 + CHUNK, DK), F32),
            pltpu.VMEM((hps * n_chunks, 8 + CHUNK, DK), F32),
        ],
        compiler_params=pltpu.CompilerParams(
            dimension_semantics=("arbitrary", "arbitrary", "arbitrary"), vmem_limit_bytes=VMEM_LIMIT),
        name="hgrn",
    )(z, z, z, z, lb_logits, gnorm.reshape(1, -1), s0, a_mat, lmask)


BIAS_SPAN = 3 * QBLK


BIAS_HEADS = 8


def _bias_kernel(rb_ref, o_ref):
    lane = lax.broadcasted_iota(jnp.int32, (1, BIAS_SPAN), 1)
    i = lax.broadcasted_iota(jnp.int32, (QBLK, QBLK), 0)
    j = lax.broadcasted_iota(jnp.int32, (QBLK, QBLK), 1)
    qc = i // CHUNK
    for h in range(BIAS_HEADS):
        rev = rb_ref[h:h + 1, :]
        far_val = rev[:, 0:1]
        row = jnp.where(lane < QBLK - 1, far_val, pltpu.roll(rev, QBLK - 1, axis=1))
        wide = jnp.broadcast_to(row, (QBLK, BIAS_SPAN))
        rolled = pltpu.roll(wide, QBLK + 1, axis=1, stride=1, stride_axis=0)
        near = (rolled[:, 0:QBLK], rolled[:, 2 * QBLK:3 * QBLK])
        far = jnp.broadcast_to(far_val, (QBLK, QBLK))
        for d in range(N_KBLK):
            kc = j // CHUNK - (QBLK // CHUNK) * d
            allowed = jnp.logical_and(kc <= qc, kc >= qc - BAND_CHUNKS)
            tile = near[d] if d < 2 else far
            o_ref[h, :, (N_KBLK - 1 - d) * QBLK:(N_KBLK - d) * QBLK] = jnp.where(allowed, tile * LOG2E, NEG_INF)


def _bias_table(rel_bias):
    heads = rel_bias.shape[0]
    assert rel_bias.shape[1] == N_REL and REL_CLIP == QBLK and heads % BIAS_HEADS == 0
    rev = jnp.pad(rel_bias[:, ::-1], ((0, 0), (0, BIAS_SPAN - N_REL)))
    return pl.pallas_call(
        _bias_kernel,
        grid=(heads // BIAS_HEADS,),
        in_specs=[pl.BlockSpec((BIAS_HEADS, BIAS_SPAN), lambda h: (h, 0))],
        out_specs=pl.BlockSpec((BIAS_HEADS, QBLK, N_KBLK * QBLK), lambda h: (h, 0, 0)),
        out_shape=jax.ShapeDtypeStruct((heads, QBLK, N_KBLK * QBLK), F32),
        compiler_params=pltpu.CompilerParams(dimension_semantics=("arbitrary",)),
        name="rel_bias_table",
    )(rev)


ROW_GROUP = 16


def _attn_scores(qm, k_ref, cols_k, bias_ref, hh, r_lo, s_ref, *, tq, clamp):
    if clamp:
        for t in range(N_KBLK):
            r = r_lo + QBLK * t
            cols = slice(QBLK * t, QBLK * (t + 1))
            kt = k_ref[pl.ds(pl.multiple_of(jnp.maximum(r, 0), QBLK), QBLK), cols_k]
            s = _dot_nt(qm, kt) + bias_ref[hh, 0:tq, cols]
            s_ref[0:tq, cols] = jnp.where(r >= 0, s, NEG_INF)
    else:
        rows = pl.ds(pl.multiple_of(r_lo, QBLK), N_KBLK * QBLK)
        s_ref[0:tq, :] = _dot_nt(qm, k_ref[rows, cols_k]) + bias_ref[hh, 0:tq, :]


def _attn_softmax(s_ref, p_ref, *, tq):
    sums = []
    for g in range(tq // ROW_GROUP):
        rows = slice(g * ROW_GROUP, (g + 1) * ROW_GROUP)
        s = s_ref[rows, :]
        p = jnp.exp2(s - jnp.max(s, axis=-1, keepdims=True))
        sums.append(jnp.sum(p, axis=-1, keepdims=True))
        p_ref[rows, :] = p.astype(BF16)
    return jnp.concatenate(sums, axis=0)


def _attn_values(p_ref, v_ref, cols_v, r_lo, *, tq, clamp):
    if clamp:
        parts = []
        for t in range(N_KBLK):
            r = pl.multiple_of(jnp.maximum(r_lo + QBLK * t, 0), QBLK)
            parts.append(_dot(p_ref[0:tq, QBLK * t:QBLK * (t + 1)], v_ref[pl.ds(r, QBLK), cols_v]))
        return functools.reduce(jnp.add, parts)
    rows = pl.ds(pl.multiple_of(r_lo, QBLK), N_KBLK * QBLK)
    return _dot(p_ref[0:tq, :], v_ref[rows, cols_v])


def _attn_kernel(q_ref, k_ref, v_ref, g_ref, bias_ref, o_ref, *scratch, tq, nqb, pps, koff):
    step = pl.program_id(2)
    n_chain = 2 * nqb * pps
    s_refs = scratch[:n_chain]
    p_refs = scratch[n_chain:]
    lane = lax.broadcasted_iota(jnp.int32, (tq, LANES), 1)
    first = lane < DH

    def compute(clamp):
        chains = [(qb, pp, hh) for qb in range(nqb) for pp in range(pps) for hh in range(2)]
        r_los = [(step * nqb + qb) * QBLK + koff - BAND_ROWS for qb in range(nqb)]
        sums, outs = {}, {}
        for step_c in range(n_chain + 2):
            if step_c < n_chain:
                qb, pp, hh = chains[step_c]
                cols = slice(pp * LANES, (pp + 1) * LANES)
                q2 = q_ref[qb * tq:(qb + 1) * tq, cols]
                qm = jnp.where(first if hh == 0 else jnp.logical_not(first), q2, jnp.zeros_like(q2))
                _attn_scores(qm, k_ref, cols, bias_ref, 2 * pp + hh, r_los[qb], s_refs[step_c],
                             tq=tq, clamp=clamp)
            if 0 <= step_c - 1 < n_chain:
                sums[step_c - 1] = _attn_softmax(s_refs[step_c - 1], p_refs[step_c - 1], tq=tq)
            if 0 <= step_c - 2 < n_chain:
                c = step_c - 2
                qb, pp, hh = chains[c]
                outs[c] = _attn_values(p_refs[c], v_ref, slice(pp * LANES, (pp + 1) * LANES), r_los[qb],
                                       tq=tq, clamp=clamp) / sums[c]
        for qb in range(nqb):
            for pp in range(pps):
                rows = slice(qb * tq, (qb + 1) * tq)
                cols = slice(pp * LANES, (pp + 1) * LANES)
                c0 = 2 * (qb * pps + pp)
                o = jnp.where(first, outs[c0], outs[c0 + 1])
                g = g_ref[rows, cols].astype(F32)
                o_ref[rows, cols] = (o * (g * _sigmoid(g))).astype(BF16)

    full_from = max(-(-(BAND_ROWS - koff) // (QBLK * nqb)), 0)
    if full_from == 0:
        compute(False)
    else:
        pl.when(step >= full_from)(lambda: compute(False))
        pl.when(step < full_from)(lambda: compute(True))


def _attention(q_arr, q_col, k_arr, k_col, v_arr, v_col, g_arr, g_col, bias, *,
               batch, q_rows, tq, nqb, pps, koff):
    k_rows = k_arr.shape[0] // batch
    ts = tq * nqb
    nq = q_rows // ts
    pairs = bias.shape[0] // 2
    assert q_rows % ts == 0 and (tq == QBLK or q_rows == tq) and pairs % pps == 0
    assert all(c % pps == 0 for c in (q_col, k_col, v_col, g_col))
    w = LANES * pps
    kern = functools.partial(_attn_kernel, tq=tq, nqb=nqb, pps=pps, koff=koff)
    return pl.pallas_call(
        kern,
        grid=(pairs // pps, batch, nq),
        in_specs=[
            pl.BlockSpec((ts, w), lambda p, b, i: (b * nq + i, q_col // pps + p)),
            pl.BlockSpec((k_rows, w), lambda p, b, i: (b, k_col // pps + p)),
            pl.BlockSpec((k_rows, w), lambda p, b, i: (b, v_col // pps + p)),
            pl.BlockSpec((ts, w), lambda p, b, i: (b * nq + i, g_col // pps + p)),
            pl.BlockSpec((2 * pps, QBLK, N_KBLK * QBLK), lambda p, b, i: (p, 0, 0)),
        ],
        out_specs=pl.BlockSpec((ts, w), lambda p, b, i: (b * nq + i, p)),
        out_shape=jax.ShapeDtypeStruct((batch * q_rows, pairs * LANES), BF16),
        scratch_shapes=([pltpu.VMEM((QBLK, N_KBLK * QBLK), F32)] * (2 * nqb * pps)
                        + [pltpu.VMEM((QBLK, N_KBLK * QBLK), BF16)] * (2 * nqb * pps)),
        compiler_params=pltpu.CompilerParams(
            dimension_semantics=("arbitrary", "arbitrary", "arbitrary"), vmem_limit_bytes=VMEM_LIMIT),
        name="band_attention",
    )(q_arr, k_arr, v_arr, g_arr, bias)


def _pack_keys_kernel(cache_ref, new_ref, o_ref):
    dh = cache_ref.shape[2]
    heads = o_ref.shape[1] // dh
    past = cache_ref.shape[1] // heads
    n_new = new_ref.shape[0]
    for h in range(heads):
        o_ref[0:past, h * dh:(h + 1) * dh] = cache_ref[0, pl.ds(h, past, stride=heads), :].astype(BF16)
    o_ref[past:past + n_new, :] = new_ref[...].astype(BF16)
    o_ref[past + n_new:, :] = jnp.zeros((o_ref.shape[0] - past - n_new, o_ref.shape[1]), BF16)


def _pack_keys(cache, new, rows_out):
    batch, past, heads, dh = cache.shape
    n_new = new.shape[0] // batch
    width = heads * dh
    return pl.pallas_call(
        _pack_keys_kernel,
        grid=(batch,),
        in_specs=[
            pl.BlockSpec((1, past * heads, dh), lambda b: (b, 0, 0)),
            pl.BlockSpec((n_new, width), lambda b: (b, 0)),
        ],
        out_specs=pl.BlockSpec((rows_out, width), lambda b: (b, 0)),
        out_shape=jax.ShapeDtypeStruct((batch * rows_out, width), BF16),
        compiler_params=pltpu.CompilerParams(dimension_semantics=("arbitrary",)),
        name="pack_keys",
    )(cache.reshape(batch, past * heads, dh), new)


MERGE_COLS = 512


def _merge_out_kernel(oa_ref, ob_ref, ma_ref, mb_ref, x_ref, wpa_ref, wpb_ref, wo_ref, gp_ref, y_ref, mg_ref):
    for n in range(x_ref.shape[1] // MERGE_COLS):
        cols = slice(n * MERGE_COLS, (n + 1) * MERGE_COLS)
        pa = _dot(oa_ref[...], wpa_ref[:, cols])
        pb = _dot(ob_ref[...], wpb_ref[:, cols])
        merged = _sigmoid(ma_ref[:, cols].astype(F32)) * pa + _sigmoid(mb_ref[:, cols].astype(F32)) * pb
        mg_ref[:, cols] = merged.astype(BF16)
    hid = _dot(mg_ref[...], wo_ref[...])
    ms = jnp.mean(hid * hid, axis=-1, keepdims=True)
    y_ref[...] = x_ref[...] + (hid * lax.rsqrt(ms + EPS)) * gp_ref[...]


def _merge_out(o_a, o_b, z, x2, wpa_bf, wpb_bf, wo_bf, norm_post, *, tm=256):
    m, d = x2.shape
    d_a = o_a.shape[1]
    d_b = o_b.shape[1]
    ma_col = (4 * d_a + 4 * d_b) // d
    resident = dict(pipeline_mode=pl.Buffered(1))
    return pl.pallas_call(
        _merge_out_kernel,
        grid=(m // tm,),
        in_specs=[
            pl.BlockSpec((tm, d_a), lambda i: (i, 0)),
            pl.BlockSpec((tm, d_b), lambda i: (i, 0)),
            pl.BlockSpec((tm, d), lambda i: (i, ma_col)),
            pl.BlockSpec((tm, d), lambda i: (i, ma_col + 1)),
            pl.BlockSpec((tm, d), lambda i: (i, 0)),
            pl.BlockSpec((d_a, d), lambda i: (0, 0), **resident),
            pl.BlockSpec((d_b, d), lambda i: (0, 0), **resident),
            pl.BlockSpec((d, d), lambda i: (0, 0), **resident),
            pl.BlockSpec((1, d), lambda i: (0, 0)),
        ],
        out_specs=pl.BlockSpec((tm, d), lambda i: (i, 0)),
        out_shape=jax.ShapeDtypeStruct((m, d), F32),
        scratch_shapes=[pltpu.VMEM((tm, d), BF16)],
        compiler_params=pltpu.CompilerParams(
            dimension_semantics=("arbitrary",), vmem_limit_bytes=VMEM_LIMIT),
        name="merge_out",
    )(o_a, o_b, z, z, x2, wpa_bf, wpb_bf, wo_bf, norm_post.reshape(1, d))


def kernel(x_prompt, x_sample, state_hgrn, cache_k, cache_v, norm_pre, w_in, lb_logits, gnorm_a,
           rel_bias, w_proj_a, w_proj_b, w_out, norm_post):
    depth = w_in.shape[0]
    assert depth == 1, "single-layer step"
    bp, tp, d = x_prompt.shape
    bs, ts, _ = x_sample.shape
    heads_a = state_hgrn.shape[2]
    heads_b = cache_k.shape[3]
    kv_rows = cache_k.shape[2]
    d_a = heads_a * DK
    d_b = heads_b * DH
    assert ts == CHUNK and kv_rows == BAND_ROWS and PAST_LEN % QBLK == 0 and tp % QBLK == 0

    w_in_bf = w_in[0].astype(BF16)
    wpa_bf = w_proj_a[0].astype(BF16)
    wpb_bf = w_proj_b[0].astype(BF16)
    wo_bf = w_out[0].astype(BF16)
    bias = _bias_table(rel_bias[0])
    col_b = 4 * d_a // LANES
    grp_b = d_b // LANES

    xp2 = x_prompt.reshape(bp * tp, d)
    rows_p = min(BAND_ROWS, tp)
    z_p, k_p, v_p = _in_proj(xp2, norm_pre[0], w_in_bf, rows_per_batch=tp, kv_rows=rows_p)
    oa_p, s_p = _hgrn(z_p, lb_logits, gnorm_a[0], jnp.zeros((bp, heads_a, DK, DK), F32),
                      batch=bp, rows_per_batch=tp, tc=4096, hps=1)
    ob_p = _attention(z_p, col_b, z_p, col_b + grp_b, z_p, col_b + 2 * grp_b, z_p, col_b + 3 * grp_b,
                      bias, batch=bp, q_rows=tp, tq=QBLK, nqb=8, pps=1, koff=0)
    y_p = _merge_out(oa_p, ob_p, z_p, xp2, wpa_bf, wpb_bf, wo_bf, norm_post[0])

    xs2 = x_sample.reshape(bs * ts, d)
    z_s, k_s, v_s = _in_proj(xs2, norm_pre[0], w_in_bf, rows_per_batch=ts, kv_rows=ts)
    oa_s, s_s = _hgrn(z_s, lb_logits, gnorm_a[0], state_hgrn[0],
                      batch=bs, rows_per_batch=ts, tc=ts, hps=heads_a)
    k_rows = kv_rows + QBLK
    k_all = _pack_keys(cache_k[0], k_s, k_rows)
    v_all = _pack_keys(cache_v[0], v_s, k_rows)
    ob_s = _attention(z_s, col_b, k_all, 0, v_all, 0,
                      z_s, col_b + 3 * grp_b, bias, batch=bs, q_rows=ts, tq=ts, nqb=1, pps=4, koff=kv_rows)
    y_s = _merge_out(oa_s, ob_s, z_s, xs2, wpa_bf, wpb_bf, wo_bf, norm_post[0])

    return (
        y_p.reshape(bp, tp, d),
        y_s.reshape(bs, ts, d),
        s_p[None],
        k_p.reshape(1, bp, rows_p, heads_b, DH),
        v_p.reshape(1, bp, rows_p, heads_b, DH),
        s_s[None],
        k_s.reshape(1, bs, ts, heads_b, DH),
        v_s.reshape(1, bs, ts, heads_b, DH),
    )
```

```python
import functools

import numpy as np
import jax
import jax.numpy as jnp
from jax import lax
from jax.experimental import pallas as pl
from jax.experimental.pallas import tpu as pltpu

F32 = jnp.float32
BF16 = jnp.bfloat16

LANES = 128
CHUNK = 64
DK = 128
DH = 64
BAND_CHUNKS = 8
BAND_ROWS = BAND_CHUNKS * CHUNK
REL_CLIP = 128
PAST_LEN = 2048
N_REL = 2 * REL_CLIP + 1
EPS = 1e-6
NEG_INF = -1e30
QBLK = 2 * CHUNK
N_KBLK = BAND_ROWS // QBLK + 1
VMEM_LIMIT = 56 * 1024 * 1024
LOG2E = 1.4426950408889634
Q_SCALE = DH ** -0.5 * LOG2E


def _dot(a, b):
    return jnp.dot(a, b, preferred_element_type=F32)


def _dot_nt(a, b):
    return lax.dot_general(a, b, (((1,), (1,)), ((), ())), preferred_element_type=F32)


def _dot_tn(a, b):
    return lax.dot_general(a, b, (((0,), (0,)), ((), ())), preferred_element_type=F32)


def _sigmoid(x):
    return 1.0 / (1.0 + jnp.exp(-x))


NORM_ROWS = 256


def _in_proj_kernel(x_ref, g_ref, cs_ref, w_ref, z_ref, k_ref, v_ref, xn_ref, *,
                    tiles_per_kv, kv_tail, k_at, v_at):
    i = pl.program_id(0)
    j = pl.program_id(1)
    tm = x_ref.shape[0]
    width = k_ref.shape[1]

    @pl.when(j == 0)
    def _():
        for r in range(0, tm, NORM_ROWS):
            rows = slice(r, r + NORM_ROWS)
            x = x_ref[rows, :]
            ms = jnp.mean(x * x, axis=-1, keepdims=True)
            xn_ref[rows, :] = ((x * lax.rsqrt(ms + EPS)) * g_ref[...]).astype(BF16)
            z_ref[rows, :] = (_dot(xn_ref[rows, :], w_ref[...]) * cs_ref[...]).astype(BF16)

    @pl.when(j != 0)
    def _():
        acc = _dot(xn_ref[...], w_ref[...])
        z_ref[...] = (acc * cs_ref[...]).astype(BF16)
        is_kv_tile = (i % tiles_per_kv) == (tiles_per_kv - 1)

        @pl.when(jnp.logical_and(is_kv_tile, j == k_at[0]))
        def _():
            k_ref[...] = acc[tm - kv_tail:, k_at[1]:k_at[1] + width]

        @pl.when(jnp.logical_and(is_kv_tile, j == v_at[0]))
        def _():
            v_ref[...] = acc[tm - kv_tail:, v_at[1]:v_at[1] + width]


def _in_proj(x2, norm_pre, w_in_bf, *, rows_per_batch, kv_rows, tm=1024, tn=1536):
    m, d = x2.shape
    d_in = w_in_bf.shape[1]
    d_b = (d_in - 2 * d) // 8
    assert m % tm == 0 and d_in % tn == 0
    if rows_per_batch >= tm:
        assert rows_per_batch % tm == 0 and kv_rows <= tm
        tiles_per_kv = rows_per_batch // tm
        kv_tail = kv_rows
    else:
        assert tm % rows_per_batch == 0 and kv_rows == rows_per_batch
        tiles_per_kv = 1
        kv_tail = tm
    n_kv = m // (tiles_per_kv * tm)
    k_at = divmod(5 * d_b, tn)
    v_at = divmod(6 * d_b, tn)
    assert k_at[1] + d_b <= tn and v_at[1] + d_b <= tn, "k / v column group must not straddle tiles"
    assert k_at[0] > 0 and v_at[0] > 0 and tm % NORM_ROWS == 0
    col = np.arange(d_in)
    col_scale = jnp.asarray(np.where((col >= 4 * d_b) & (col < 5 * d_b), Q_SCALE, 1.0)[None, :], F32)
    kern = functools.partial(_in_proj_kernel, tiles_per_kv=tiles_per_kv, kv_tail=kv_tail,
                             k_at=k_at, v_at=v_at)
    kv_spec = pl.BlockSpec((kv_tail, d_b), lambda i, j: (i // tiles_per_kv, 0))
    return pl.pallas_call(
        kern,
        grid=(m // tm, d_in // tn),
        in_specs=[
            pl.BlockSpec((tm, d), lambda i, j: (i, 0)),
            pl.BlockSpec((1, d), lambda i, j: (0, 0)),
            pl.BlockSpec((1, tn), lambda i, j: (0, j)),
            pl.BlockSpec((d, tn), lambda i, j: (0, j)),
        ],
        out_specs=[pl.BlockSpec((tm, tn), lambda i, j: (i, j)), kv_spec, kv_spec],
        out_shape=[
            jax.ShapeDtypeStruct((m, d_in), BF16),
            jax.ShapeDtypeStruct((n_kv * kv_tail, d_b), F32),
            jax.ShapeDtypeStruct((n_kv * kv_tail, d_b), F32),
        ],
        scratch_shapes=[pltpu.VMEM((tm, d), BF16)],
        compiler_params=pltpu.CompilerParams(
            dimension_semantics=("arbitrary", "arbitrary"), vmem_limit_bytes=VMEM_LIMIT),
        name="in_proj",
    )(x2, norm_pre.reshape(1, d), col_scale, w_in_bf)


HGRN_DIAG = 4
HGRN_LEVELS = tuple(m for m in (1, 2, 4, 8, 16, 32) if m >= HGRN_DIAG)


def _hgrn_sum_matrix():
    t = np.arange(CHUNK)[:, None]
    u = np.arange(CHUNK)[None, :]
    groups = [u <= t, u > t]
    for m in HGRN_LEVELS:
        ref = (t // (2 * m)) * 2 * m + m - 1
        groups.append(np.where(t > ref, (u > ref) & (u <= t), (u > t) & (u <= ref)))
    groups.append((u <= t) & (u // HGRN_DIAG == t // HGRN_DIAG))
    a = np.concatenate(groups, axis=0).astype(np.float32)
    return np.concatenate([a, a], axis=1)


def _hgrn_level_masks():
    t = np.arange(CHUNK)[:, None]
    s = np.arange(CHUNK)[None, :]
    masks = []
    for m in HGRN_LEVELS[:-1]:
        same = (t // (2 * m)) == (s // (2 * m))
        if m < 8:
            same = same & (t % (2 * m) >= m) & (s % (2 * m) < m)
        masks.append(same)
    return np.stack(masks).astype(np.float32)


def _hgrn_chunk(zq, zf, v, lb, oml, a2, st, level_masks, diag_masks, kpad_ref, cpad_ref):
    q = zq * _sigmoid(zq)
    sg = _sigmoid(zf)
    k = oml * (1.0 - sg)
    logf = jnp.log2(lb + oml * sg)

    hi = logf.astype(BF16)
    lo = (logf - hi.astype(F32)).astype(BF16)
    c = _dot(a2, jnp.concatenate([hi, lo], axis=0))
    b = c[0:CHUNK]
    b_rev = c[CHUNK:2 * CHUNK]
    c_loc = c[(2 + len(HGRN_LEVELS)) * CHUNK:]

    o_inter = _dot_nt((q * jnp.exp2(b)).astype(BF16), st.astype(BF16))
    kd = (k * jnp.exp2(b_rev)).astype(BF16)
    st_new = st * jnp.exp2(b[CHUNK - 1:CHUNK, :]) + _dot_tn(v, kd)

    zero8 = jnp.zeros((8, DK), F32)
    sc = None
    for lvl, m in enumerate(HGRN_LEVELS):
        zdec = jnp.exp2(c[(2 + lvl) * CHUNK:(3 + lvl) * CHUNK])
        if m < 8:
            q_s = q * zdec
            k_s = k * zdec
        else:
            q_rows, k_rows = [], []
            for g in range(CHUNK // 8):
                rows = slice(8 * g, 8 * g + 8)
                if (8 * g) % (2 * m) >= m:
                    q_rows.append(q[rows] * zdec[rows])
                    k_rows.append(zero8)
                else:
                    q_rows.append(zero8)
                    k_rows.append(k[rows] * zdec[rows])
            q_s = jnp.concatenate(q_rows, axis=0)
            k_s = jnp.concatenate(k_rows, axis=0)
        part = _dot_nt(q_s.astype(BF16), k_s.astype(BF16))
        if m in level_masks:
            part = part * level_masks[m]
        sc = part if sc is None else sc + part

    kpad_ref[8:, :] = k
    cpad_ref[8:, :] = c_loc
    diag = []
    for blk in range(CHUNK // 8):
        r0 = 8 * blk
        qb = q[r0:r0 + 8]
        cb = c_loc[r0:r0 + 8]
        acc = jnp.where(diag_masks[0], jnp.sum(qb * k[r0:r0 + 8], axis=-1, keepdims=True), 0.0)
        for dl in range(1, HGRN_DIAG):
            ks = kpad_ref[8 + r0 - dl:16 + r0 - dl, :]
            cs = cpad_ref[8 + r0 - dl:16 + r0 - dl, :]
            dec = jnp.exp2(jnp.minimum(cb - cs, 0.0))
            col = jnp.sum((qb * ks) * dec, axis=-1, keepdims=True)
            acc = jnp.where(diag_masks[dl], col, acc)
        diag.append(pltpu.roll(acc, r0, axis=1) if blk else acc)
    sc = sc + jnp.concatenate(diag, axis=0)[:, :CHUNK]

    o = o_inter + _dot(sc.astype(BF16), v)
    return o, st_new


def _hgrn_kernel(zq_ref, zf_ref, zi_ref, zg_ref, lbl_ref, gn_ref, s0_ref, a_ref, lm_ref,
                 o_ref, sfin_ref, st_ref, kpad_ref, cpad_ref, *, n_chunks, hps):
    c_idx = pl.program_id(2)

    @pl.when(c_idx == 0)
    def _():
        for hh in range(hps):
            st_ref[hh] = s0_ref[0, hh].T

    a2 = a_ref[...]
    level_masks = {m: lm_ref[i] for i, m in enumerate(HGRN_LEVELS[:-1])}
    sub = lax.broadcasted_iota(jnp.int32, (8, DK), 0)
    lane = lax.broadcasted_iota(jnp.int32, (8, DK), 1)
    diag_masks = [jnp.logical_and(lane == sub - dl, sub % HGRN_DIAG >= dl) for dl in range(HGRN_DIAG)]

    kpad_ref[:, 0:8, :] = jnp.zeros((hps * n_chunks, 8, DK), F32)
    cpad_ref[:, 0:8, :] = jnp.zeros((hps * n_chunks, 8, DK), F32)

    lbl = lbl_ref[...]
    e = jnp.exp(lbl - jnp.max(lbl, axis=0, keepdims=True))
    lb_all = e[0:1] / jnp.sum(e, axis=0, keepdims=True)

    state = [st_ref[hh] for hh in range(hps)]
    for slot, (hh, ci) in enumerate([(hh, ci) for hh in range(hps) for ci in range(n_chunks)]):
        rows, cols = slice(ci * CHUNK, (ci + 1) * CHUNK), slice(hh * DK, (hh + 1) * DK)
        lb = lb_all[:, cols]
        zg = zg_ref[rows, cols].astype(F32)
        o, state[hh] = _hgrn_chunk(zq_ref[rows, cols].astype(F32), zf_ref[rows, cols].astype(F32),
                                   zi_ref[rows, cols], lb, 1.0 - lb, a2, state[hh], level_masks,
                                   diag_masks, kpad_ref.at[slot], cpad_ref.at[slot])
        o = o * lax.rsqrt(jnp.mean(o * o, axis=-1, keepdims=True) + EPS)
        o_ref[rows, cols] = ((o * gn_ref[:, cols]) * (zg * _sigmoid(zg))).astype(BF16)
    for hh in range(hps):
        st_ref[hh] = state[hh]

    @pl.when(c_idx == pl.num_programs(2) - 1)
    def _():
        for hh in range(hps):
            sfin_ref[0, hh] = st_ref[hh].T


def _hgrn(z, lb_logits, gnorm, s0, *, batch, rows_per_batch, tc, hps):
    m = z.shape[0]
    heads = s0.shape[1]
    nt = rows_per_batch // tc
    n_chunks = tc // CHUNK
    hg = heads // hps
    assert heads % hps == 0
    a_mat = jnp.asarray(_hgrn_sum_matrix(), BF16)
    lmask = jnp.asarray(_hgrn_level_masks(), F32)
    w = DK * hps

    def zspec(group):
        return pl.BlockSpec((tc, w), lambda b, h, c: (b * nt + c, group * hg + h))

    kern = functools.partial(_hgrn_kernel, n_chunks=n_chunks, hps=hps)
    return pl.pallas_call(
        kern,
        grid=(batch, hg, nt),
        in_specs=[
            zspec(0), zspec(1), zspec(2), zspec(3),
            pl.BlockSpec((lb_logits.shape[0], w), lambda b, h, c: (0, h)),
            pl.BlockSpec((1, w), lambda b, h, c: (0, h)),
            pl.BlockSpec((1, hps, DK, DK), lambda b, h, c: (b, h, 0, 0)),
            pl.BlockSpec(a_mat.shape, lambda b, h, c: (0, 0)),
            pl.BlockSpec(lmask.shape, lambda b, h, c: (0, 0, 0)),
        ],
        out_specs=[
            pl.BlockSpec((tc, w), lambda b, h, c: (b * nt + c, h)),
            pl.BlockSpec((1, hps, DK, DK), lambda b, h, c: (b, h, 0, 0)),
        ],
        out_shape=[
            jax.ShapeDtypeStruct((m, heads * DK), BF16),
            jax.ShapeDtypeStruct(s0.shape, F32),
        ],
        scratch_shapes=[
            pltpu.VMEM((hps, DK, DK), F32),
            pltpu.VMEM((hps * n_chunks, 8 + CHUNK, DK), F32),
            pltpu.VMEM((hps * n_chunks, 8 + CHUNK, DK), F32),
        ],
        compiler_params=pltpu.CompilerParams(
            dimension_semantics=("arbitrary", "arbitrary", "arbitrary"), vmem_limit_bytes=VMEM_LIMIT),
        name="hgrn",
    )(z, z, z, z, lb_logits, gnorm.reshape(1, -1), s0, a_mat, lmask)


BIAS_SPAN = 3 * QBLK


BIAS_HEADS = 8


def _bias_kernel(rb_ref, o_ref):
    lane = lax.broadcasted_iota(jnp.int32, (1, BIAS_SPAN), 1)
    i = lax.broadcasted_iota(jnp.int32, (QBLK, QBLK), 0)
    j = lax.broadcasted_iota(jnp.int32, (QBLK, QBLK), 1)
    qc = i // CHUNK
    for h in range(BIAS_HEADS):
        rev = rb_ref[h:h + 1, :]
        far_val = rev[:, 0:1]
        row = jnp.where(lane < QBLK - 1, far_val, pltpu.roll(rev, QBLK - 1, axis=1))
        wide = jnp.broadcast_to(row, (QBLK, BIAS_SPAN))
        rolled = pltpu.roll(wide, QBLK + 1, axis=1, stride=1, stride_axis=0)
        near = (rolled[:, 0:QBLK], rolled[:, 2 * QBLK:3 * QBLK])
        far = jnp.broadcast_to(far_val, (QBLK, QBLK))
        for d in range(N_KBLK):
            kc = j // CHUNK - (QBLK // CHUNK) * d
            allowed = jnp.logical_and(kc <= qc, kc >= qc - BAND_CHUNKS)
            tile = near[d] if d < 2 else far
            o_ref[h, :, (N_KBLK - 1 - d) * QBLK:(N_KBLK - d) * QBLK] = jnp.where(allowed, tile * LOG2E, NEG_INF)


def _bias_table(rel_bias):
    heads = rel_bias.shape[0]
    assert rel_bias.shape[1] == N_REL and REL_CLIP == QBLK and heads % BIAS_HEADS == 0
    rev = jnp.pad(rel_bias[:, ::-1], ((0, 0), (0, BIAS_SPAN - N_REL)))
    return pl.pallas_call(
        _bias_kernel,
        grid=(heads // BIAS_HEADS,),
        in_specs=[pl.BlockSpec((BIAS_HEADS, BIAS_SPAN), lambda h: (h, 0))],
        out_specs=pl.BlockSpec((BIAS_HEADS, QBLK, N_KBLK * QBLK), lambda h: (h, 0, 0)),
        out_shape=jax.ShapeDtypeStruct((heads, QBLK, N_KBLK * QBLK), F32),
        compiler_params=pltpu.CompilerParams(dimension_semantics=("arbitrary",)),
        name="rel_bias_table",
    )(rev)


ROW_GROUP = 16


def _attn_scores(qm, k_ref, cols_k, bias_ref, hh, r_lo, s_ref, *, tq, clamp):
    if clamp:
        for t in range(N_KBLK):
            r = r_lo + QBLK * t
            cols = slice(QBLK * t, QBLK * (t + 1))
            kt = k_ref[pl.ds(pl.multiple_of(jnp.maximum(r, 0), QBLK), QBLK), cols_k]
            s = _dot_nt(qm, kt) + bias_ref[hh, 0:tq, cols]
            s_ref[0:tq, cols] = jnp.where(r >= 0, s, NEG_INF)
    else:
        rows = pl.ds(pl.multiple_of(r_lo, QBLK), N_KBLK * QBLK)
        s_ref[0:tq, :] = _dot_nt(qm, k_ref[rows, cols_k]) + bias_ref[hh, 0:tq, :]


def _attn_softmax(s_ref, p_ref, *, tq):
    sums = []
    for g in range(tq // ROW_GROUP):
        rows = slice(g * ROW_GROUP, (g + 1) * ROW_GROUP)
        s = s_ref[rows, :]
        p = jnp.exp2(s - jnp.max(s, axis=-1, keepdims=True))
        sums.append(jnp.sum(p, axis=-1, keepdims=True))
        p_ref[rows, :] = p.astype(BF16)
    return jnp.concatenate(sums, axis=0)


def _attn_values(p_ref, v_ref, cols_v, r_lo, *, tq, clamp):
    if clamp:
        parts = []
        for t in range(N_KBLK):
            r = pl.multiple_of(jnp.maximum(r_lo + QBLK * t, 0), QBLK)
            parts.append(_dot(p_ref[0:tq, QBLK * t:QBLK * (t + 1)], v_ref[pl.ds(r, QBLK), cols_v]))
        return functools.reduce(jnp.add, parts)
    rows = pl.ds(pl.multiple_of(r_lo, QBLK), N_KBLK * QBLK)
    return _dot(p_ref[0:tq, :], v_ref[rows, cols_v])


def _attn_kernel(q_ref, k_ref, v_ref, g_ref, bias_ref, o_ref, *scratch, tq, nqb, pps, koff):
    step = pl.program_id(2)
    n_chain = 2 * nqb * pps
    s_refs = scratch[:n_chain]
    p_refs = scratch[n_chain:]
    lane = lax.broadcasted_iota(jnp.int32, (tq, LANES), 1)
    first = lane < DH

    def compute(clamp):
        chains = [(qb, pp, hh) for qb in range(nqb) for pp in range(pps) for hh in range(2)]
        r_los = [(step * nqb + qb) * QBLK + koff - BAND_ROWS for qb in range(nqb)]
        sums, outs = {}, {}
        for step_c in range(n_chain + 2):
            if step_c < n_chain:
                qb, pp, hh = chains[step_c]
                cols = slice(pp * LANES, (pp + 1) * LANES)
                q2 = q_ref[qb * tq:(qb + 1) * tq, cols]
                qm = jnp.where(first if hh == 0 else jnp.logical_not(first), q2, jnp.zeros_like(q2))
                _attn_scores(qm, k_ref, cols, bias_ref, 2 * pp + hh, r_los[qb], s_refs[step_c],
                             tq=tq, clamp=clamp)
            if 0 <= step_c - 1 < n_chain:
                sums[step_c - 1] = _attn_softmax(s_refs[step_c - 1], p_refs[step_c - 1], tq=tq)
            if 0 <= step_c - 2 < n_chain:
                c = step_c - 2
                qb, pp, hh = chains[c]
                outs[c] = _attn_values(p_refs[c], v_ref, slice(pp * LANES, (pp + 1) * LANES), r_los[qb],
                                       tq=tq, clamp=clamp) / sums[c]
        for qb in range(nqb):
            for pp in range(pps):
                rows = slice(qb * tq, (qb + 1) * tq)
                cols = slice(pp * LANES, (pp + 1) * LANES)
                c0 = 2 * (qb * pps + pp)
                o = jnp.where(first, outs[c0], outs[c0 + 1])
                g = g_ref[rows, cols].astype(F32)
                o_ref[rows, cols] = (o * (g * _sigmoid(g))).astype(BF16)

    full_from = max(-(-(BAND_ROWS - koff) // (QBLK * nqb)), 0)
    if full_from == 0:
        compute(False)
    else:
        pl.when(step >= full_from)(lambda: compute(False))
        pl.when(step < full_from)(lambda: compute(True))


def _attention(q_arr, q_col, k_arr, k_col, v_arr, v_col, g_arr, g_col, bias, *,
               batch, q_rows, tq, nqb, pps, koff):
    k_rows = k_arr.shape[0] // batch
    ts = tq * nqb
    nq = q_rows // ts
    pairs = bias.shape[0] // 2
    assert q_rows % ts == 0 and (tq == QBLK or q_rows == tq) and pairs % pps == 0
    assert all(c % pps == 0 for c in (q_col, k_col, v_col, g_col))
    w = LANES * pps
    kern = functools.partial(_attn_kernel, tq=tq, nqb=nqb, pps=pps, koff=koff)
    return pl.pallas_call(
        kern,
        grid=(pairs // pps, batch, nq),
        in_specs=[
            pl.BlockSpec((ts, w), lambda p, b, i: (b * nq + i, q_col // pps + p)),
            pl.BlockSpec((k_rows, w), lambda p, b, i: (b, k_col // pps + p)),
            pl.BlockSpec((k_rows, w), lambda p, b, i: (b, v_col // pps + p)),
            pl.BlockSpec((ts, w), lambda p, b, i: (b * nq + i, g_col // pps + p)),
            pl.BlockSpec((2 * pps, QBLK, N_KBLK * QBLK), lambda p, b, i: (p, 0, 0)),
        ],
        out_specs=pl.BlockSpec((ts, w), lambda p, b, i: (b * nq + i, p)),
        out_shape=jax.ShapeDtypeStruct((batch * q_rows, pairs * LANES), BF16),
        scratch_shapes=([pltpu.VMEM((QBLK, N_KBLK * QBLK), F32)] * (2 * nqb * pps)
                        + [pltpu.VMEM((QBLK, N_KBLK * QBLK), BF16)] * (2 * nqb * pps)),
        compiler_params=pltpu.CompilerParams(
            dimension_semantics=("arbitrary", "arbitrary", "arbitrary"), vmem_limit_bytes=VMEM_LIMIT),
        name="band_attention",
    )(q_arr, k_arr, v_arr, g_arr, bias)


MERGE_COLS = 512


MERGE_ROWS = 256


def _merge_out_kernel(oa_ref, ob_ref, ma_ref, mb_ref, x_ref, wpa_ref, wpb_ref, wo_ref, gp_ref, y_ref, mg_ref):
    tm, d = x_ref.shape
    passes = [slice(r, r + MERGE_ROWS) for r in range(0, tm, MERGE_ROWS)]
    for rows in passes:
        for n in range(d // MERGE_COLS):
            cols = slice(n * MERGE_COLS, (n + 1) * MERGE_COLS)
            pa = _dot(oa_ref[rows, :], wpa_ref[:, cols])
            pb = _dot(ob_ref[rows, :], wpb_ref[:, cols])
            merged = (_sigmoid(ma_ref[rows, cols].astype(F32)) * pa
                      + _sigmoid(mb_ref[rows, cols].astype(F32)) * pb)
            mg_ref[rows, cols] = merged.astype(BF16)
    for rows in passes:
        hid = _dot(mg_ref[rows, :], wo_ref[...])
        ms = jnp.mean(hid * hid, axis=-1, keepdims=True)
        y_ref[rows, :] = x_ref[rows, :] + (hid * lax.rsqrt(ms + EPS)) * gp_ref[...]


def _merge_out(o_a, o_b, z, x2, wpa_bf, wpb_bf, wo_bf, norm_post, *, tm=512):
    m, d = x2.shape
    d_a = o_a.shape[1]
    d_b = o_b.shape[1]
    ma_col = (4 * d_a + 4 * d_b) // d
    resident = dict(pipeline_mode=pl.Buffered(1))
    return pl.pallas_call(
        _merge_out_kernel,
        grid=(m // tm,),
        in_specs=[
            pl.BlockSpec((tm, d_a), lambda i: (i, 0)),
            pl.BlockSpec((tm, d_b), lambda i: (i, 0)),
            pl.BlockSpec((tm, d), lambda i: (i, ma_col)),
            pl.BlockSpec((tm, d), lambda i: (i, ma_col + 1)),
            pl.BlockSpec((tm, d), lambda i: (i, 0)),
            pl.BlockSpec((d_a, d), lambda i: (0, 0), **resident),
            pl.BlockSpec((d_b, d), lambda i: (0, 0), **resident),
            pl.BlockSpec((d, d), lambda i: (0, 0), **resident),
            pl.BlockSpec((1, d), lambda i: (0, 0)),
        ],
        out_specs=pl.BlockSpec((tm, d), lambda i: (i, 0)),
        out_shape=jax.ShapeDtypeStruct((m, d), F32),
        scratch_shapes=[pltpu.VMEM((tm, d), BF16)],
        compiler_params=pltpu.CompilerParams(
            dimension_semantics=("arbitrary",), vmem_limit_bytes=VMEM_LIMIT),
        name="merge_out",
    )(o_a, o_b, z, z, x2, wpa_bf, wpb_bf, wo_bf, norm_post.reshape(1, d))


def kernel(x_prompt, x_sample, state_hgrn, cache_k, cache_v, norm_pre, w_in, lb_logits, gnorm_a,
           rel_bias, w_proj_a, w_proj_b, w_out, norm_post):
    depth = w_in.shape[0]
    assert depth == 1, "single-layer step"
    bp, tp, d = x_prompt.shape
    bs, ts, _ = x_sample.shape
    heads_a = state_hgrn.shape[2]
    heads_b = cache_k.shape[3]
    kv_rows = cache_k.shape[2]
    d_a = heads_a * DK
    d_b = heads_b * DH
    assert ts == CHUNK and kv_rows == BAND_ROWS and PAST_LEN % QBLK == 0 and tp % QBLK == 0

    w_in_bf = w_in[0].astype(BF16)
    wpa_bf = w_proj_a[0].astype(BF16)
    wpb_bf = w_proj_b[0].astype(BF16)
    wo_bf = w_out[0].astype(BF16)
    bias = _bias_table(rel_bias[0])
    col_b = 4 * d_a // LANES
    grp_b = d_b // LANES

    xp2 = x_prompt.reshape(bp * tp, d)
    rows_p = min(BAND_ROWS, tp)
    z_p, k_p, v_p = _in_proj(xp2, norm_pre[0], w_in_bf, rows_per_batch=tp, kv_rows=rows_p)
    oa_p, s_p = _hgrn(z_p, lb_logits, gnorm_a[0], jnp.zeros((bp, heads_a, DK, DK), F32),
                      batch=bp, rows_per_batch=tp, tc=4096, hps=1)
    ob_p = _attention(z_p, col_b, z_p, col_b + grp_b, z_p, col_b + 2 * grp_b, z_p, col_b + 3 * grp_b,
                      bias, batch=bp, q_rows=tp, tq=QBLK, nqb=16, pps=1, koff=0)
    y_p = _merge_out(oa_p, ob_p, z_p, xp2, wpa_bf, wpb_bf, wo_bf, norm_post[0])

    xs2 = x_sample.reshape(bs * ts, d)
    z_s, k_s, v_s = _in_proj(xs2, norm_pre[0], w_in_bf, rows_per_batch=ts, kv_rows=ts)
    oa_s, s_s = _hgrn(z_s, lb_logits, gnorm_a[0], state_hgrn[0],
                      batch=bs, rows_per_batch=ts, tc=ts, hps=heads_a)
    pad = jnp.zeros((bs, QBLK - ts, d_b), BF16)
    k_all = jnp.concatenate([cache_k[0].astype(BF16).reshape(bs, kv_rows, d_b),
                             k_s.reshape(bs, ts, d_b).astype(BF16), pad], axis=1)
    v_all = jnp.concatenate([cache_v[0].astype(BF16).reshape(bs, kv_rows, d_b),
                             v_s.reshape(bs, ts, d_b).astype(BF16), pad], axis=1)
    k_rows = kv_rows + QBLK
    ob_s = _attention(z_s, col_b, k_all.reshape(bs * k_rows, d_b), 0, v_all.reshape(bs * k_rows, d_b), 0,
                      z_s, col_b + 3 * grp_b, bias, batch=bs, q_rows=ts, tq=ts, nqb=1, pps=4, koff=kv_rows)
    y_s = _merge_out(oa_s, ob_s, z_s, xs2, wpa_bf, wpb_bf, wo_bf, norm_post[0])

    return (
        y_p.reshape(bp, tp, d),
        y_s.reshape(bs, ts, d),
        s_p[None],
        k_p.reshape(1, bp, rows_p, heads_b, DH),
        v_p.reshape(1, bp, rows_p, heads_b, DH),
        s_s[None],
        k_s.reshape(1, bs, ts, heads_b, DH),
        v_s.reshape(1, bs, ts, heads_b, DH),
    )
```

```python
import functools

import numpy as np
import jax
import jax.numpy as jnp
from jax import lax
from jax.experimental import pallas as pl
from jax.experimental.pallas import tpu as pltpu

F32 = jnp.float32
BF16 = jnp.bfloat16

LANES = 128
CHUNK = 64
DK = 128
DH = 64
BAND_CHUNKS = 8
BAND_ROWS = BAND_CHUNKS * CHUNK
REL_CLIP = 128
PAST_LEN = 2048
N_REL = 2 * REL_CLIP + 1
EPS = 1e-6
NEG_INF = -1e30
QBLK = 2 * CHUNK
N_KBLK = BAND_ROWS // QBLK + 1
VMEM_LIMIT = 56 * 1024 * 1024
LOG2E = 1.4426950408889634
Q_SCALE = DH ** -0.5 * LOG2E


def _dot(a, b):
    return jnp.dot(a, b, preferred_element_type=F32)


def _dot_nt(a, b):
    return lax.dot_general(a, b, (((1,), (1,)), ((), ())), preferred_element_type=F32)


def _dot_tn(a, b):
    return lax.dot_general(a, b, (((0,), (0,)), ((), ())), preferred_element_type=F32)


def _sigmoid(x):
    return 1.0 / (1.0 + jnp.exp(-x))


NORM_ROWS = 256


def _in_proj_kernel(x_ref, g_ref, cs_ref, w_ref, z_ref, k_ref, v_ref, xn_ref, *,
                    tiles_per_kv, kv_tail, k_at, v_at):
    i = pl.program_id(0)
    j = pl.program_id(1)
    tm = x_ref.shape[0]
    width = k_ref.shape[1]

    @pl.when(j == 0)
    def _():
        for r in range(0, tm, NORM_ROWS):
            rows = slice(r, r + NORM_ROWS)
            x = x_ref[rows, :]
            ms = jnp.mean(x * x, axis=-1, keepdims=True)
            xn_ref[rows, :] = ((x * lax.rsqrt(ms + EPS)) * g_ref[...]).astype(BF16)
            z_ref[rows, :] = (_dot(xn_ref[rows, :], w_ref[...]) * cs_ref[...]).astype(BF16)

    @pl.when(j != 0)
    def _():
        acc = _dot(xn_ref[...], w_ref[...])
        z_ref[...] = (acc * cs_ref[...]).astype(BF16)
        is_kv_tile = (i % tiles_per_kv) == (tiles_per_kv - 1)

        @pl.when(jnp.logical_and(is_kv_tile, j == k_at[0]))
        def _():
            k_ref[...] = acc[tm - kv_tail:, k_at[1]:k_at[1] + width]

        @pl.when(jnp.logical_and(is_kv_tile, j == v_at[0]))
        def _():
            v_ref[...] = acc[tm - kv_tail:, v_at[1]:v_at[1] + width]


def _in_proj(x2, norm_pre, w_in_bf, *, rows_per_batch, kv_rows, tm=1024, tn=1536):
    m, d = x2.shape
    d_in = w_in_bf.shape[1]
    d_b = (d_in - 2 * d) // 8
    assert m % tm == 0 and d_in % tn == 0
    if rows_per_batch >= tm:
        assert rows_per_batch % tm == 0 and kv_rows <= tm
        tiles_per_kv = rows_per_batch // tm
        kv_tail = kv_rows
    else:
        assert tm % rows_per_batch == 0 and kv_rows == rows_per_batch
        tiles_per_kv = 1
        kv_tail = tm
    n_kv = m // (tiles_per_kv * tm)
    k_at = divmod(5 * d_b, tn)
    v_at = divmod(6 * d_b, tn)
    assert k_at[1] + d_b <= tn and v_at[1] + d_b <= tn, "k / v column group must not straddle tiles"
    assert k_at[0] > 0 and v_at[0] > 0 and tm % NORM_ROWS == 0
    col = np.arange(d_in)
    col_scale = jnp.asarray(np.where((col >= 4 * d_b) & (col < 5 * d_b), Q_SCALE, 1.0)[None, :], F32)
    kern = functools.partial(_in_proj_kernel, tiles_per_kv=tiles_per_kv, kv_tail=kv_tail,
                             k_at=k_at, v_at=v_at)
    kv_spec = pl.BlockSpec((kv_tail, d_b), lambda i, j: (i // tiles_per_kv, 0))
    return pl.pallas_call(
        kern,
        grid=(m // tm, d_in // tn),
        in_specs=[
            pl.BlockSpec((tm, d), lambda i, j: (i, 0)),
            pl.BlockSpec((1, d), lambda i, j: (0, 0)),
            pl.BlockSpec((1, tn), lambda i, j: (0, j)),
            pl.BlockSpec((d, tn), lambda i, j: (0, j)),
        ],
        out_specs=[pl.BlockSpec((tm, tn), lambda i, j: (i, j)), kv_spec, kv_spec],
        out_shape=[
            jax.ShapeDtypeStruct((m, d_in), BF16),
            jax.ShapeDtypeStruct((n_kv * kv_tail, d_b), F32),
            jax.ShapeDtypeStruct((n_kv * kv_tail, d_b), F32),
        ],
        scratch_shapes=[pltpu.VMEM((tm, d), BF16)],
        compiler_params=pltpu.CompilerParams(
            dimension_semantics=("arbitrary", "arbitrary"), vmem_limit_bytes=VMEM_LIMIT),
        name="in_proj",
    )(x2, norm_pre.reshape(1, d), col_scale, w_in_bf)


HGRN_DIAG = 4
HGRN_LEVELS = tuple(m for m in (1, 2, 4, 8, 16, 32) if m >= HGRN_DIAG)


def _hgrn_sum_matrix():
    t = np.arange(CHUNK)[:, None]
    u = np.arange(CHUNK)[None, :]
    groups = [u <= t, u > t]
    for m in HGRN_LEVELS:
        ref = (t // (2 * m)) * 2 * m + m - 1
        groups.append(np.where(t > ref, (u > ref) & (u <= t), (u > t) & (u <= ref)))
    groups.append((u <= t) & (u // HGRN_DIAG == t // HGRN_DIAG))
    a = np.concatenate(groups, axis=0).astype(np.float32)
    return np.concatenate([a, a], axis=1)


def _hgrn_level_masks():
    t = np.arange(CHUNK)[:, None]
    s = np.arange(CHUNK)[None, :]
    masks = []
    for m in HGRN_LEVELS[:-1]:
        same = (t // (2 * m)) == (s // (2 * m))
        if m < 8:
            same = same & (t % (2 * m) >= m) & (s % (2 * m) < m)
        masks.append(same)
    return np.stack(masks).astype(np.float32)


def _hgrn_chunk(zq, zf, v, lb, oml, a2, st, level_masks, diag_masks, kpad_ref, cpad_ref):
    q = zq * _sigmoid(zq)
    sg = _sigmoid(zf)
    k = oml * (1.0 - sg)
    logf = jnp.log2(lb + oml * sg)

    hi = logf.astype(BF16)
    lo = (logf - hi.astype(F32)).astype(BF16)
    c = _dot(a2, jnp.concatenate([hi, lo], axis=0))
    b = c[0:CHUNK]
    b_rev = c[CHUNK:2 * CHUNK]
    c_loc = c[(2 + len(HGRN_LEVELS)) * CHUNK:]

    o_inter = _dot_nt((q * jnp.exp2(b)).astype(BF16), st.astype(BF16))
    kd = (k * jnp.exp2(b_rev)).astype(BF16)
    st_new = st * jnp.exp2(b[CHUNK - 1:CHUNK, :]) + _dot_tn(v, kd)

    zero8 = jnp.zeros((8, DK), F32)
    sc = None
    for lvl, m in enumerate(HGRN_LEVELS):
        zdec = jnp.exp2(c[(2 + lvl) * CHUNK:(3 + lvl) * CHUNK])
        if m < 8:
            q_s = q * zdec
            k_s = k * zdec
        else:
            q_rows, k_rows = [], []
            for g in range(CHUNK // 8):
                rows = slice(8 * g, 8 * g + 8)
                if (8 * g) % (2 * m) >= m:
                    q_rows.append(q[rows] * zdec[rows])
                    k_rows.append(zero8)
                else:
                    q_rows.append(zero8)
                    k_rows.append(k[rows] * zdec[rows])
            q_s = jnp.concatenate(q_rows, axis=0)
            k_s = jnp.concatenate(k_rows, axis=0)
        part = _dot_nt(q_s.astype(BF16), k_s.astype(BF16))
        if m in level_masks:
            part = part * level_masks[m]
        sc = part if sc is None else sc + part

    kpad_ref[8:, :] = k
    cpad_ref[8:, :] = c_loc
    diag = []
    for blk in range(CHUNK // 8):
        r0 = 8 * blk
        qb = q[r0:r0 + 8]
        cb = c_loc[r0:r0 + 8]
        acc = jnp.where(diag_masks[0], jnp.sum(qb * k[r0:r0 + 8], axis=-1, keepdims=True), 0.0)
        for dl in range(1, HGRN_DIAG):
            ks = kpad_ref[8 + r0 - dl:16 + r0 - dl, :]
            cs = cpad_ref[8 + r0 - dl:16 + r0 - dl, :]
            dec = jnp.exp2(jnp.minimum(cb - cs, 0.0))
            col = jnp.sum((qb * ks) * dec, axis=-1, keepdims=True)
            acc = jnp.where(diag_masks[dl], col, acc)
        diag.append(pltpu.roll(acc, r0, axis=1) if blk else acc)
    sc = sc + jnp.concatenate(diag, axis=0)[:, :CHUNK]

    o = o_inter + _dot(sc.astype(BF16), v)
    return o, st_new


def _hgrn_kernel(zq_ref, zf_ref, zi_ref, zg_ref, lbl_ref, gn_ref, s0_ref, a_ref, lm_ref,
                 o_ref, sfin_ref, st_ref, kpad_ref, cpad_ref, *, n_chunks, hps):
    c_idx = pl.program_id(2)

    @pl.when(c_idx == 0)
    def _():
        for hh in range(hps):
            st_ref[hh] = s0_ref[0, hh].T

    a2 = a_ref[...]
    level_masks = {m: lm_ref[i] for i, m in enumerate(HGRN_LEVELS[:-1])}
    sub = lax.broadcasted_iota(jnp.int32, (8, DK), 0)
    lane = lax.broadcasted_iota(jnp.int32, (8, DK), 1)
    diag_masks = [jnp.logical_and(lane == sub - dl, sub % HGRN_DIAG >= dl) for dl in range(HGRN_DIAG)]

    kpad_ref[:, 0:8, :] = jnp.zeros((hps * n_chunks, 8, DK), F32)
    cpad_ref[:, 0:8, :] = jnp.zeros((hps * n_chunks, 8, DK), F32)

    lbl = lbl_ref[...]
    e = jnp.exp(lbl - jnp.max(lbl, axis=0, keepdims=True))
    lb_all = e[0:1] / jnp.sum(e, axis=0, keepdims=True)

    state = [st_ref[hh] for hh in range(hps)]
    for slot, (hh, ci) in enumerate([(hh, ci) for hh in range(hps) for ci in range(n_chunks)]):
        rows, cols = slice(ci * CHUNK, (ci + 1) * CHUNK), slice(hh * DK, (hh + 1) * DK)
        lb = lb_all[:, cols]
        zg = zg_ref[rows, cols].astype(F32)
        o, state[hh] = _hgrn_chunk(zq_ref[rows, cols].astype(F32), zf_ref[rows, cols].astype(F32),
                                   zi_ref[rows, cols], lb, 1.0 - lb, a2, state[hh], level_masks,
                                   diag_masks, kpad_ref.at[slot], cpad_ref.at[slot])
        o = o * lax.rsqrt(jnp.mean(o * o, axis=-1, keepdims=True) + EPS)
        o_ref[rows, cols] = ((o * gn_ref[:, cols]) * (zg * _sigmoid(zg))).astype(BF16)
    for hh in range(hps):
        st_ref[hh] = state[hh]

    @pl.when(c_idx == pl.num_programs(2) - 1)
    def _():
        for hh in range(hps):
            sfin_ref[0, hh] = st_ref[hh].T


def _hgrn(z, lb_logits, gnorm, s0, *, batch, rows_per_batch, tc, hps):
    m = z.shape[0]
    heads = s0.shape[1]
    nt = rows_per_batch // tc
    n_chunks = tc // CHUNK
    hg = heads // hps
    assert heads % hps == 0
    a_mat = jnp.asarray(_hgrn_sum_matrix(), BF16)
    lmask = jnp.asarray(_hgrn_level_masks(), F32)
    w = DK * hps

    def zspec(group):
        return pl.BlockSpec((tc, w), lambda b, h, c: (b * nt + c, group * hg + h))

    kern = functools.partial(_hgrn_kernel, n_chunks=n_chunks, hps=hps)
    return pl.pallas_call(
        kern,
        grid=(batch, hg, nt),
        in_specs=[
            zspec(0), zspec(1), zspec(2), zspec(3),
            pl.BlockSpec((lb_logits.shape[0], w), lambda b, h, c: (0, h)),
            pl.BlockSpec((1, w), lambda b, h, c: (0, h)),
            pl.BlockSpec((1, hps, DK, DK), lambda b, h, c: (b, h, 0, 0)),
            pl.BlockSpec(a_mat.shape, lambda b, h, c: (0, 0)),
            pl.BlockSpec(lmask.shape, lambda b, h, c: (0, 0, 0)),
        ],
        out_specs=[
            pl.BlockSpec((tc, w), lambda b, h, c: (b * nt + c, h)),
            pl.BlockSpec((1, hps, DK, DK), lambda b, h, c: (b, h, 0, 0)),
        ],
        out_shape=[
            jax.ShapeDtypeStruct((m, heads * DK), BF16),
            jax.ShapeDtypeStruct(s0.shape, F32),
        ],
        scratch_shapes=[
            pltpu.VMEM((hps, DK, DK), F32),
            pltpu.VMEM((hps * n_chunks, 8 + CHUNK, DK), F32),
            pltpu.VMEM((hps * n_chunks, 8 + CHUNK, DK), F32),
        ],
        compiler_params=pltpu.CompilerParams(
            dimension_semantics=("arbitrary", "arbitrary", "arbitrary"), vmem_limit_bytes=VMEM_LIMIT),
        name="hgrn",
    )(z, z, z, z, lb_logits, gnorm.reshape(1, -1), s0, a_mat, lmask)


BIAS_SPAN = 3 * QBLK


BIAS_HEADS = 8


def _bias_kernel(rb_ref, o_ref):
    lane = lax.broadcasted_iota(jnp.int32, (1, BIAS_SPAN), 1)
    i = lax.broadcasted_iota(jnp.int32, (QBLK, QBLK), 0)
    j = lax.broadcasted_iota(jnp.int32, (QBLK, QBLK), 1)
    qc = i // CHUNK
    for h in range(BIAS_HEADS):
        rev = rb_ref[h:h + 1, :]
        far_val = rev[:, 0:1]
        row = jnp.where(lane < QBLK - 1, far_val, pltpu.roll(rev, QBLK - 1, axis=1))
        wide = jnp.broadcast_to(row, (QBLK, BIAS_SPAN))
        rolled = pltpu.roll(wide, QBLK + 1, axis=1, stride=1, stride_axis=0)
        near = (rolled[:, 0:QBLK], rolled[:, 2 * QBLK:3 * QBLK])
        far = jnp.broadcast_to(far_val, (QBLK, QBLK))
        for d in range(N_KBLK):
            kc = j // CHUNK - (QBLK // CHUNK) * d
            allowed = jnp.logical_and(kc <= qc, kc >= qc - BAND_CHUNKS)
            tile = near[d] if d < 2 else far
            o_ref[h, :, (N_KBLK - 1 - d) * QBLK:(N_KBLK - d) * QBLK] = jnp.where(allowed, tile * LOG2E, NEG_INF)


def _bias_table(rel_bias):
    heads = rel_bias.shape[0]
    assert rel_bias.shape[1] == N_REL and REL_CLIP == QBLK and heads % BIAS_HEADS == 0
    rev = jnp.pad(rel_bias[:, ::-1], ((0, 0), (0, BIAS_SPAN - N_REL)))
    return pl.pallas_call(
        _bias_kernel,
        grid=(heads // BIAS_HEADS,),
        in_specs=[pl.BlockSpec((BIAS_HEADS, BIAS_SPAN), lambda h: (h, 0))],
        out_specs=pl.BlockSpec((BIAS_HEADS, QBLK, N_KBLK * QBLK), lambda h: (h, 0, 0)),
        out_shape=jax.ShapeDtypeStruct((heads, QBLK, N_KBLK * QBLK), F32),
        compiler_params=pltpu.CompilerParams(dimension_semantics=("arbitrary",)),
        name="rel_bias_table",
    )(rev)


ROW_GROUP = 16


def _attn_scores(qm, k_ref, cols_k, bias_ref, hh, r_lo, s_ref, *, tq, clamp):
    if clamp:
        for t in range(N_KBLK):
            r = r_lo + QBLK * t
            cols = slice(QBLK * t, QBLK * (t + 1))
            kt = k_ref[pl.ds(pl.multiple_of(jnp.maximum(r, 0), QBLK), QBLK), cols_k]
            s = _dot_nt(qm, kt) + bias_ref[hh, 0:tq, cols]
            s_ref[0:tq, cols] = jnp.where(r >= 0, s, NEG_INF)
    else:
        rows = pl.ds(pl.multiple_of(r_lo, QBLK), N_KBLK * QBLK)
        s_ref[0:tq, :] = _dot_nt(qm, k_ref[rows, cols_k]) + bias_ref[hh, 0:tq, :]


def _attn_softmax(s_ref, p_ref, *, tq):
    sums = []
    for g in range(tq // ROW_GROUP):
        rows = slice(g * ROW_GROUP, (g + 1) * ROW_GROUP)
        s = s_ref[rows, :]
        p = jnp.exp2(s - jnp.max(s, axis=-1, keepdims=True))
        sums.append(jnp.sum(p, axis=-1, keepdims=True))
        p_ref[rows, :] = p.astype(BF16)
    return jnp.concatenate(sums, axis=0)


def _attn_values(p_ref, v_ref, cols_v, r_lo, *, tq, clamp):
    if clamp:
        parts = []
        for t in range(N_KBLK):
            r = pl.multiple_of(jnp.maximum(r_lo + QBLK * t, 0), QBLK)
            parts.append(_dot(p_ref[0:tq, QBLK * t:QBLK * (t + 1)], v_ref[pl.ds(r, QBLK), cols_v]))
        return functools.reduce(jnp.add, parts)
    rows = pl.ds(pl.multiple_of(r_lo, QBLK), N_KBLK * QBLK)
    return _dot(p_ref[0:tq, :], v_ref[rows, cols_v])


def _attn_kernel(q_ref, k_ref, v_ref, g_ref, bias_ref, o_ref, *scratch, tq, nqb, pps, koff):
    step = pl.program_id(2)
    n_chain = 2 * nqb * pps
    s_refs = scratch[:n_chain]
    p_refs = scratch[n_chain:]
    lane = lax.broadcasted_iota(jnp.int32, (tq, LANES), 1)
    first = lane < DH

    full_from = max(-(-(BAND_ROWS - koff) // (QBLK * nqb)), 0)

    def compute(early):
        chains = [(qb, pp, hh) for qb in range(nqb) for pp in range(pps) for hh in range(2)]
        r_los = [(step * nqb + qb) * QBLK + koff - BAND_ROWS for qb in range(nqb)]
        clamps = [early and (full_from > 1 or qb * QBLK + koff < BAND_ROWS) for qb in range(nqb)]
        sums, outs = {}, {}
        for step_c in range(n_chain + 2):
            if step_c < n_chain:
                qb, pp, hh = chains[step_c]
                cols = slice(pp * LANES, (pp + 1) * LANES)
                q2 = q_ref[qb * tq:(qb + 1) * tq, cols]
                qm = jnp.where(first if hh == 0 else jnp.logical_not(first), q2, jnp.zeros_like(q2))
                _attn_scores(qm, k_ref, cols, bias_ref, 2 * pp + hh, r_los[qb], s_refs[step_c],
                             tq=tq, clamp=clamps[qb])
            if 0 <= step_c - 1 < n_chain:
                sums[step_c - 1] = _attn_softmax(s_refs[step_c - 1], p_refs[step_c - 1], tq=tq)
            if 0 <= step_c - 2 < n_chain:
                c = step_c - 2
                qb, pp, hh = chains[c]
                outs[c] = _attn_values(p_refs[c], v_ref, slice(pp * LANES, (pp + 1) * LANES), r_los[qb],
                                       tq=tq, clamp=clamps[qb]) / sums[c]
        for qb in range(nqb):
            for pp in range(pps):
                rows = slice(qb * tq, (qb + 1) * tq)
                cols = slice(pp * LANES, (pp + 1) * LANES)
                c0 = 2 * (qb * pps + pp)
                o = jnp.where(first, outs[c0], outs[c0 + 1])
                g = g_ref[rows, cols].astype(F32)
                o_ref[rows, cols] = (o * (g * _sigmoid(g))).astype(BF16)

    if full_from == 0:
        compute(False)
    else:
        pl.when(step >= full_from)(lambda: compute(False))
        pl.when(step < full_from)(lambda: compute(True))


def _attention(q_arr, q_col, k_arr, k_col, v_arr, v_col, g_arr, g_col, bias, *,
               batch, q_rows, tq, nqb, pps, koff):
    k_rows = k_arr.shape[0] // batch
    ts = tq * nqb
    nq = q_rows // ts
    pairs = bias.shape[0] // 2
    assert q_rows % ts == 0 and (tq == QBLK or q_rows == tq) and pairs % pps == 0
    assert all(c % pps == 0 for c in (q_col, k_col, v_col, g_col))
    w = LANES * pps
    kern = functools.partial(_attn_kernel, tq=tq, nqb=nqb, pps=pps, koff=koff)
    return pl.pallas_call(
        kern,
        grid=(pairs // pps, batch, nq),
        in_specs=[
            pl.BlockSpec((ts, w), lambda p, b, i: (b * nq + i, q_col // pps + p)),
            pl.BlockSpec((k_rows, w), lambda p, b, i: (b, k_col // pps + p)),
            pl.BlockSpec((k_rows, w), lambda p, b, i: (b, v_col // pps + p)),
            pl.BlockSpec((ts, w), lambda p, b, i: (b * nq + i, g_col // pps + p)),
            pl.BlockSpec((2 * pps, QBLK, N_KBLK * QBLK), lambda p, b, i: (p, 0, 0)),
        ],
        out_specs=pl.BlockSpec((ts, w), lambda p, b, i: (b * nq + i, p)),
        out_shape=jax.ShapeDtypeStruct((batch * q_rows, pairs * LANES), BF16),
        scratch_shapes=([pltpu.VMEM((QBLK, N_KBLK * QBLK), F32)] * (2 * nqb * pps)
                        + [pltpu.VMEM((QBLK, N_KBLK * QBLK), BF16)] * (2 * nqb * pps)),
        compiler_params=pltpu.CompilerParams(
            dimension_semantics=("arbitrary", "arbitrary", "arbitrary"), vmem_limit_bytes=VMEM_LIMIT),
        name="band_attention",
    )(q_arr, k_arr, v_arr, g_arr, bias)


MERGE_COLS = 512


MERGE_ROWS = 256


def _merge_out_kernel(oa_ref, ob_ref, ma_ref, mb_ref, x_ref, wpa_ref, wpb_ref, wo_ref, gp_ref, y_ref, mg_ref):
    tm, d = x_ref.shape
    passes = [slice(r, r + MERGE_ROWS) for r in range(0, tm, MERGE_ROWS)]
    for rows in passes:
        for n in range(d // MERGE_COLS):
            cols = slice(n * MERGE_COLS, (n + 1) * MERGE_COLS)
            pa = _dot(oa_ref[rows, :], wpa_ref[:, cols])
            pb = _dot(ob_ref[rows, :], wpb_ref[:, cols])
            merged = (_sigmoid(ma_ref[rows, cols].astype(F32)) * pa
                      + _sigmoid(mb_ref[rows, cols].astype(F32)) * pb)
            mg_ref[rows, cols] = merged.astype(BF16)
    for rows in passes:
        hid = _dot(mg_ref[rows, :], wo_ref[...])
        ms = jnp.mean(hid * hid, axis=-1, keepdims=True)
        y_ref[rows, :] = x_ref[rows, :] + (hid * lax.rsqrt(ms + EPS)) * gp_ref[...]


def _merge_out(o_a, o_b, z, x2, wpa_bf, wpb_bf, wo_bf, norm_post, *, tm=512):
    m, d = x2.shape
    d_a = o_a.shape[1]
    d_b = o_b.shape[1]
    ma_col = (4 * d_a + 4 * d_b) // d
    resident = dict(pipeline_mode=pl.Buffered(1))
    return pl.pallas_call(
        _merge_out_kernel,
        grid=(m // tm,),
        in_specs=[
            pl.BlockSpec((tm, d_a), lambda i: (i, 0)),
            pl.BlockSpec((tm, d_b), lambda i: (i, 0)),
            pl.BlockSpec((tm, d), lambda i: (i, ma_col)),
            pl.BlockSpec((tm, d), lambda i: (i, ma_col + 1)),
            pl.BlockSpec((tm, d), lambda i: (i, 0)),
            pl.BlockSpec((d_a, d), lambda i: (0, 0), **resident),
            pl.BlockSpec((d_b, d), lambda i: (0, 0), **resident),
            pl.BlockSpec((d, d), lambda i: (0, 0), **resident),
            pl.BlockSpec((1, d), lambda i: (0, 0)),
        ],
        out_specs=pl.BlockSpec((tm, d), lambda i: (i, 0)),
        out_shape=jax.ShapeDtypeStruct((m, d), F32),
        scratch_shapes=[pltpu.VMEM((tm, d), BF16)],
        compiler_params=pltpu.CompilerParams(
            dimension_semantics=("arbitrary",), vmem_limit_bytes=VMEM_LIMIT),
        name="merge_out",
    )(o_a, o_b, z, z, x2, wpa_bf, wpb_bf, wo_bf, norm_post.reshape(1, d))


def kernel(x_prompt, x_sample, state_hgrn, cache_k, cache_v, norm_pre, w_in, lb_logits, gnorm_a,
           rel_bias, w_proj_a, w_proj_b, w_out, norm_post):
    depth = w_in.shape[0]
    assert depth == 1, "single-layer step"
    bp, tp, d = x_prompt.shape
    bs, ts, _ = x_sample.shape
    heads_a = state_hgrn.shape[2]
    heads_b = cache_k.shape[3]
    kv_rows = cache_k.shape[2]
    d_a = heads_a * DK
    d_b = heads_b * DH
    assert ts == CHUNK and kv_rows == BAND_ROWS and PAST_LEN % QBLK == 0 and tp % QBLK == 0

    w_in_bf = w_in[0].astype(BF16)
    wpa_bf = w_proj_a[0].astype(BF16)
    wpb_bf = w_proj_b[0].astype(BF16)
    wo_bf = w_out[0].astype(BF16)
    bias = _bias_table(rel_bias[0])
    col_b = 4 * d_a // LANES
    grp_b = d_b // LANES

    xp2 = x_prompt.reshape(bp * tp, d)
    rows_p = min(BAND_ROWS, tp)
    z_p, k_p, v_p = _in_proj(xp2, norm_pre[0], w_in_bf, rows_per_batch=tp, kv_rows=rows_p)
    oa_p, s_p = _hgrn(z_p, lb_logits, gnorm_a[0], jnp.zeros((bp, heads_a, DK, DK), F32),
                      batch=bp, rows_per_batch=tp, tc=4096, hps=1)
    ob_p = _attention(z_p, col_b, z_p, col_b + grp_b, z_p, col_b + 2 * grp_b, z_p, col_b + 3 * grp_b,
                      bias, batch=bp, q_rows=tp, tq=QBLK, nqb=8, pps=1, koff=0)
    y_p = _merge_out(oa_p, ob_p, z_p, xp2, wpa_bf, wpb_bf, wo_bf, norm_post[0])

    xs2 = x_sample.reshape(bs * ts, d)
    z_s, k_s, v_s = _in_proj(xs2, norm_pre[0], w_in_bf, rows_per_batch=ts, kv_rows=ts)
    oa_s, s_s = _hgrn(z_s, lb_logits, gnorm_a[0], state_hgrn[0],
                      batch=bs, rows_per_batch=ts, tc=ts, hps=heads_a)
    pad = jnp.zeros((bs, QBLK - ts, d_b), BF16)
    k_all = jnp.concatenate([cache_k[0].astype(BF16).reshape(bs, kv_rows, d_b),
                             k_s.reshape(bs, ts, d_b).astype(BF16), pad], axis=1)
    v_all = jnp.concatenate([cache_v[0].astype(BF16).reshape(bs, kv_rows, d_b),
                             v_s.reshape(bs, ts, d_b).astype(BF16), pad], axis=1)
    k_rows = kv_rows + QBLK
    ob_s = _attention(z_s, col_b, k_all.reshape(bs * k_rows, d_b), 0, v_all.reshape(bs * k_rows, d_b), 0,
                      z_s, col_b + 3 * grp_b, bias, batch=bs, q_rows=ts, tq=ts, nqb=1, pps=4, koff=kv_rows)
    y_s = _merge_out(oa_s, ob_s, z_s, xs2, wpa_bf, wpb_bf, wo_bf, norm_post[0])

    return (
        y_p.reshape(bp, tp, d),
        y_s.reshape(bs, ts, d),
        s_p[None],
        k_p.reshape(1, bp, rows_p, heads_b, DH),
        v_p.reshape(1, bp, rows_p, heads_b, DH),
        s_s[None],
        k_s.reshape(1, bs, ts, heads_b, DH),
        v_s.reshape(1, bs, ts, heads_b, DH),
    )
```

```python
import functools

import numpy as np
import jax
import jax.numpy as jnp
from jax import lax
from jax.experimental import pallas as pl
from jax.experimental.pallas import tpu as pltpu

F32 = jnp.float32
BF16 = jnp.bfloat16

LANES = 128
CHUNK = 64
DK = 128
DH = 64
BAND_CHUNKS = 8
BAND_ROWS = BAND_CHUNKS * CHUNK
REL_CLIP = 128
PAST_LEN = 2048
N_REL = 2 * REL_CLIP + 1
EPS = 1e-6
NEG_INF = -1e30
QBLK = 2 * CHUNK
N_KBLK = BAND_ROWS // QBLK + 1
VMEM_LIMIT = 56 * 1024 * 1024
LOG2E = 1.4426950408889634
Q_SCALE = DH ** -0.5 * LOG2E


def _dot(a, b):
    return jnp.dot(a, b, preferred_element_type=F32)


def _dot_nt(a, b):
    return lax.dot_general(a, b, (((1,), (1,)), ((), ())), preferred_element_type=F32)


def _dot_tn(a, b):
    return lax.dot_general(a, b, (((0,), (0,)), ((), ())), preferred_element_type=F32)


def _sigmoid(x):
    return 1.0 / (1.0 + jnp.exp(-x))


NORM_ROWS = 256


def _in_proj_kernel(x_ref, g_ref, cs_ref, w_ref, z_ref, k_ref, v_ref, xn_ref, *,
                    tiles_per_kv, kv_tail, k_at, v_at):
    i = pl.program_id(0)
    j = pl.program_id(1)
    tm = x_ref.shape[0]
    width = k_ref.shape[1]

    @pl.when(j == 0)
    def _():
        for r in range(0, tm, NORM_ROWS):
            rows = slice(r, r + NORM_ROWS)
            x = x_ref[rows, :]
            ms = jnp.mean(x * x, axis=-1, keepdims=True)
            xn_ref[rows, :] = ((x * lax.rsqrt(ms + EPS)) * g_ref[...]).astype(BF16)
            z_ref[rows, :] = (_dot(xn_ref[rows, :], w_ref[...]) * cs_ref[...]).astype(BF16)

    @pl.when(j != 0)
    def _():
        acc = _dot(xn_ref[...], w_ref[...])
        z_ref[...] = (acc * cs_ref[...]).astype(BF16)
        is_kv_tile = (i % tiles_per_kv) == (tiles_per_kv - 1)

        @pl.when(jnp.logical_and(is_kv_tile, j == k_at[0]))
        def _():
            k_ref[...] = acc[tm - kv_tail:, k_at[1]:k_at[1] + width]

        @pl.when(jnp.logical_and(is_kv_tile, j == v_at[0]))
        def _():
            v_ref[...] = acc[tm - kv_tail:, v_at[1]:v_at[1] + width]


def _in_proj(x2, norm_pre, w_in_bf, *, rows_per_batch, kv_rows, tm=1024, tn=1536):
    m, d = x2.shape
    d_in = w_in_bf.shape[1]
    d_b = (d_in - 2 * d) // 8
    assert m % tm == 0 and d_in % tn == 0
    if rows_per_batch >= tm:
        assert rows_per_batch % tm == 0 and kv_rows <= tm
        tiles_per_kv = rows_per_batch // tm
        kv_tail = kv_rows
    else:
        assert tm % rows_per_batch == 0 and kv_rows == rows_per_batch
        tiles_per_kv = 1
        kv_tail = tm
    n_kv = m // (tiles_per_kv * tm)
    k_at = divmod(5 * d_b, tn)
    v_at = divmod(6 * d_b, tn)
    assert k_at[1] + d_b <= tn and v_at[1] + d_b <= tn, "k / v column group must not straddle tiles"
    assert k_at[0] > 0 and v_at[0] > 0 and tm % NORM_ROWS == 0
    col = np.arange(d_in)
    col_scale = jnp.asarray(np.where((col >= 4 * d_b) & (col < 5 * d_b), Q_SCALE, 1.0)[None, :], F32)
    kern = functools.partial(_in_proj_kernel, tiles_per_kv=tiles_per_kv, kv_tail=kv_tail,
                             k_at=k_at, v_at=v_at)
    kv_spec = pl.BlockSpec((kv_tail, d_b), lambda i, j: (i // tiles_per_kv, 0))
    return pl.pallas_call(
        kern,
        grid=(m // tm, d_in // tn),
        in_specs=[
            pl.BlockSpec((tm, d), lambda i, j: (i, 0)),
            pl.BlockSpec((1, d), lambda i, j: (0, 0)),
            pl.BlockSpec((1, tn), lambda i, j: (0, j)),
            pl.BlockSpec((d, tn), lambda i, j: (0, j)),
        ],
        out_specs=[pl.BlockSpec((tm, tn), lambda i, j: (i, j)), kv_spec, kv_spec],
        out_shape=[
            jax.ShapeDtypeStruct((m, d_in), BF16),
            jax.ShapeDtypeStruct((n_kv * kv_tail, d_b), F32),
            jax.ShapeDtypeStruct((n_kv * kv_tail, d_b), F32),
        ],
        scratch_shapes=[pltpu.VMEM((tm, d), BF16)],
        compiler_params=pltpu.CompilerParams(
            dimension_semantics=("arbitrary", "arbitrary"), vmem_limit_bytes=VMEM_LIMIT),
        name="in_proj",
    )(x2, norm_pre.reshape(1, d), col_scale, w_in_bf)


HGRN_DIAG = 4
HGRN_LEVELS = tuple(m for m in (1, 2, 4, 8, 16, 32) if m >= HGRN_DIAG)


def _hgrn_sum_matrix():
    t = np.arange(CHUNK)[:, None]
    u = np.arange(CHUNK)[None, :]
    groups = [u <= t, u > t]
    for m in HGRN_LEVELS:
        ref = (t // (2 * m)) * 2 * m + m - 1
        groups.append(np.where(t > ref, (u > ref) & (u <= t), (u > t) & (u <= ref)))
    groups.append((u <= t) & (u // HGRN_DIAG == t // HGRN_DIAG))
    a = np.concatenate(groups, axis=0).astype(np.float32)
    return np.concatenate([a, a], axis=1)


def _hgrn_level_masks():
    t = np.arange(CHUNK)[:, None]
    s = np.arange(CHUNK)[None, :]
    masks = []
    for m in HGRN_LEVELS[:-1]:
        same = (t // (2 * m)) == (s // (2 * m))
        if m < 8:
            same = same & (t % (2 * m) >= m) & (s % (2 * m) < m)
        masks.append(same)
    return np.stack(masks).astype(np.float32)


def _hgrn_chunk(zq, zf, v, lb, oml, a2, st, level_masks, diag_masks, kpad_ref, cpad_ref):
    q = zq * _sigmoid(zq)
    sg = _sigmoid(zf)
    k = oml * (1.0 - sg)
    logf = jnp.log2(lb + oml * sg)

    hi = logf.astype(BF16)
    lo = (logf - hi.astype(F32)).astype(BF16)
    c = _dot(a2, jnp.concatenate([hi, lo], axis=0))
    b = c[0:CHUNK]
    b_rev = c[CHUNK:2 * CHUNK]
    c_loc = c[(2 + len(HGRN_LEVELS)) * CHUNK:]

    o_inter = _dot_nt((q * jnp.exp2(b)).astype(BF16), st.astype(BF16))
    kd = (k * jnp.exp2(b_rev)).astype(BF16)
    st_new = st * jnp.exp2(b[CHUNK - 1:CHUNK, :]) + _dot_tn(v, kd)

    zero8 = jnp.zeros((8, DK), F32)
    sc = None
    for lvl, m in enumerate(HGRN_LEVELS):
        zdec = jnp.exp2(c[(2 + lvl) * CHUNK:(3 + lvl) * CHUNK])
        if m < 8:
            q_s = q * zdec
            k_s = k * zdec
        else:
            q_rows, k_rows = [], []
            for g in range(CHUNK // 8):
                rows = slice(8 * g, 8 * g + 8)
                if (8 * g) % (2 * m) >= m:
                    q_rows.append(q[rows] * zdec[rows])
                    k_rows.append(zero8)
                else:
                    q_rows.append(zero8)
                    k_rows.append(k[rows] * zdec[rows])
            q_s = jnp.concatenate(q_rows, axis=0)
            k_s = jnp.concatenate(k_rows, axis=0)
        part = _dot_nt(q_s.astype(BF16), k_s.astype(BF16))
        if m in level_masks:
            part = part * level_masks[m]
        sc = part if sc is None else sc + part

    kpad_ref[8:, :] = k
    cpad_ref[8:, :] = c_loc
    diag = []
    for blk in range(CHUNK // 8):
        r0 = 8 * blk
        qb = q[r0:r0 + 8]
        cb = c_loc[r0:r0 + 8]
        acc = jnp.where(diag_masks[0], jnp.sum(qb * k[r0:r0 + 8], axis=-1, keepdims=True), 0.0)
        for dl in range(1, HGRN_DIAG):
            ks = kpad_ref[8 + r0 - dl:16 + r0 - dl, :]
            cs = cpad_ref[8 + r0 - dl:16 + r0 - dl, :]
            dec = jnp.exp2(jnp.minimum(cb - cs, 0.0))
            col = jnp.sum((qb * ks) * dec, axis=-1, keepdims=True)
            acc = jnp.where(diag_masks[dl], col, acc)
        diag.append(pltpu.roll(acc, r0, axis=1) if blk else acc)
    sc = sc + jnp.concatenate(diag, axis=0)[:, :CHUNK]

    o = o_inter + _dot(sc.astype(BF16), v)
    return o, st_new


def _hgrn_kernel(zq_ref, zf_ref, zi_ref, zg_ref, lbl_ref, gn_ref, s0_ref, a_ref, lm_ref,
                 o_ref, sfin_ref, st_ref, kpad_ref, cpad_ref, *, n_chunks, hps):
    c_idx = pl.program_id(2)

    @pl.when(c_idx == 0)
    def _():
        for hh in range(hps):
            st_ref[hh] = s0_ref[0, hh].T

    a2 = a_ref[...]
    level_masks = {m: lm_ref[i] for i, m in enumerate(HGRN_LEVELS[:-1])}
    sub = lax.broadcasted_iota(jnp.int32, (8, DK), 0)
    lane = lax.broadcasted_iota(jnp.int32, (8, DK), 1)
    diag_masks = [jnp.logical_and(lane == sub - dl, sub % HGRN_DIAG >= dl) for dl in range(HGRN_DIAG)]

    kpad_ref[:, 0:8, :] = jnp.zeros((hps * n_chunks, 8, DK), F32)
    cpad_ref[:, 0:8, :] = jnp.zeros((hps * n_chunks, 8, DK), F32)

    lbl = lbl_ref[...]
    e = jnp.exp(lbl - jnp.max(lbl, axis=0, keepdims=True))
    lb_all = e[0:1] / jnp.sum(e, axis=0, keepdims=True)

    state = [st_ref[hh] for hh in range(hps)]
    for slot, (hh, ci) in enumerate([(hh, ci) for hh in range(hps) for ci in range(n_chunks)]):
        rows, cols = slice(ci * CHUNK, (ci + 1) * CHUNK), slice(hh * DK, (hh + 1) * DK)
        lb = lb_all[:, cols]
        zg = zg_ref[rows, cols].astype(F32)
        o, state[hh] = _hgrn_chunk(zq_ref[rows, cols].astype(F32), zf_ref[rows, cols].astype(F32),
                                   zi_ref[rows, cols], lb, 1.0 - lb, a2, state[hh], level_masks,
                                   diag_masks, kpad_ref.at[slot], cpad_ref.at[slot])
        o = o * lax.rsqrt(jnp.mean(o * o, axis=-1, keepdims=True) + EPS)
        o_ref[rows, cols] = ((o * gn_ref[:, cols]) * (zg * _sigmoid(zg))).astype(BF16)
    for hh in range(hps):
        st_ref[hh] = state[hh]

    @pl.when(c_idx == pl.num_programs(2) - 1)
    def _():
        for hh in range(hps):
            sfin_ref[0, hh] = st_ref[hh].T


def _hgrn(z, lb_logits, gnorm, s0, *, batch, rows_per_batch, tc, hps):
    m = z.shape[0]
    heads = s0.shape[1]
    nt = rows_per_batch // tc
    n_chunks = tc // CHUNK
    hg = heads // hps
    assert heads % hps == 0
    a_mat = jnp.asarray(_hgrn_sum_matrix(), BF16)
    lmask = jnp.asarray(_hgrn_level_masks(), F32)
    w = DK * hps

    def zspec(group):
        return pl.BlockSpec((tc, w), lambda b, h, c: (b * nt + c, group * hg + h))

    kern = functools.partial(_hgrn_kernel, n_chunks=n_chunks, hps=hps)
    return pl.pallas_call(
        kern,
        grid=(batch, hg, nt),
        in_specs=[
            zspec(0), zspec(1), zspec(2), zspec(3),
            pl.BlockSpec((lb_logits.shape[0], w), lambda b, h, c: (0, h)),
            pl.BlockSpec((1, w), lambda b, h, c: (0, h)),
            pl.BlockSpec((1, hps, DK, DK), lambda b, h, c: (b, h, 0, 0)),
            pl.BlockSpec(a_mat.shape, lambda b, h, c: (0, 0)),
            pl.BlockSpec(lmask.shape, lambda b, h, c: (0, 0, 0)),
        ],
        out_specs=[
            pl.BlockSpec((tc, w), lambda b, h, c: (b * nt + c, h)),
            pl.BlockSpec((1, hps, DK, DK), lambda b, h, c: (b, h, 0, 0)),
        ],
        out_shape=[
            jax.ShapeDtypeStruct((m, heads * DK), BF16),
            jax.ShapeDtypeStruct(s0.shape, F32),
        ],
        scratch_shapes=[
            pltpu.VMEM((hps, DK, DK), F32),
            pltpu.VMEM((hps * n_chunks, 8 + CHUNK, DK), F32),
            pltpu.VMEM((hps * n_chunks, 8 + CHUNK, DK), F32),
        ],
        compiler_params=pltpu.CompilerParams(
            dimension_semantics=("arbitrary", "arbitrary", "arbitrary"), vmem_limit_bytes=VMEM_LIMIT),
        name="hgrn",
    )(z, z, z, z, lb_logits, gnorm.reshape(1, -1), s0, a_mat, lmask)


BIAS_SPAN = 3 * QBLK


BIAS_HEADS = 8


def _bias_kernel(rb_ref, o_ref):
    lane = lax.broadcasted_iota(jnp.int32, (1, BIAS_SPAN), 1)
    i = lax.broadcasted_iota(jnp.int32, (QBLK, QBLK), 0)
    j = lax.broadcasted_iota(jnp.int32, (QBLK, QBLK), 1)
    qc = i // CHUNK
    for h in range(BIAS_HEADS):
        rev = rb_ref[h:h + 1, :]
        far_val = rev[:, 0:1]
        row = jnp.where(lane < QBLK - 1, far_val, pltpu.roll(rev, QBLK - 1, axis=1))
        wide = jnp.broadcast_to(row, (QBLK, BIAS_SPAN))
        rolled = pltpu.roll(wide, QBLK + 1, axis=1, stride=1, stride_axis=0)
        near = (rolled[:, 0:QBLK], rolled[:, 2 * QBLK:3 * QBLK])
        far = jnp.broadcast_to(far_val, (QBLK, QBLK))
        for d in range(N_KBLK):
            kc = j // CHUNK - (QBLK // CHUNK) * d
            allowed = jnp.logical_and(kc <= qc, kc >= qc - BAND_CHUNKS)
            tile = near[d] if d < 2 else far
            o_ref[h, :, (N_KBLK - 1 - d) * QBLK:(N_KBLK - d) * QBLK] = jnp.where(allowed, tile * LOG2E, NEG_INF)


def _bias_table(rel_bias):
    heads = rel_bias.shape[0]
    assert rel_bias.shape[1] == N_REL and REL_CLIP == QBLK and heads % BIAS_HEADS == 0
    rev = jnp.pad(rel_bias[:, ::-1], ((0, 0), (0, BIAS_SPAN - N_REL)))
    return pl.pallas_call(
        _bias_kernel,
        grid=(heads // BIAS_HEADS,),
        in_specs=[pl.BlockSpec((BIAS_HEADS, BIAS_SPAN), lambda h: (h, 0))],
        out_specs=pl.BlockSpec((BIAS_HEADS, QBLK, N_KBLK * QBLK), lambda h: (h, 0, 0)),
        out_shape=jax.ShapeDtypeStruct((heads, QBLK, N_KBLK * QBLK), F32),
        compiler_params=pltpu.CompilerParams(dimension_semantics=("arbitrary",)),
        name="rel_bias_table",
    )(rev)


ROW_GROUP = 16


def _attn_scores(qm, k_ref, cols_k, bias_ref, hh, r_lo, s_ref, *, tq, clamp):
    if clamp and isinstance(r_lo, int):
        for t in range(N_KBLK):
            r = r_lo + QBLK * t
            cols = slice(QBLK * t, QBLK * (t + 1))
            if r < 0:
                s_ref[0:tq, cols] = jnp.full((tq, QBLK), NEG_INF, F32)
            else:
                s_ref[0:tq, cols] = _dot_nt(qm, k_ref[r:r + QBLK, cols_k]) + bias_ref[hh, 0:tq, cols]
    elif clamp:
        for t in range(N_KBLK):
            r = r_lo + QBLK * t
            cols = slice(QBLK * t, QBLK * (t + 1))
            kt = k_ref[pl.ds(pl.multiple_of(jnp.maximum(r, 0), QBLK), QBLK), cols_k]
            s = _dot_nt(qm, kt) + bias_ref[hh, 0:tq, cols]
            s_ref[0:tq, cols] = jnp.where(r >= 0, s, NEG_INF)
    else:
        rows = pl.ds(pl.multiple_of(r_lo, QBLK), N_KBLK * QBLK)
        s_ref[0:tq, :] = _dot_nt(qm, k_ref[rows, cols_k]) + bias_ref[hh, 0:tq, :]


def _attn_softmax(s_ref, p_ref, *, tq):
    sums = []
    for g in range(tq // ROW_GROUP):
        rows = slice(g * ROW_GROUP, (g + 1) * ROW_GROUP)
        s = s_ref[rows, :]
        p = jnp.exp2(s - jnp.max(s, axis=-1, keepdims=True))
        sums.append(jnp.sum(p, axis=-1, keepdims=True))
        p_ref[rows, :] = p.astype(BF16)
    return jnp.concatenate(sums, axis=0)


def _attn_values(p_ref, v_ref, cols_v, r_lo, *, tq, clamp):
    if clamp:
        parts = []
        for t in range(N_KBLK):
            r = r_lo + QBLK * t
            if isinstance(r_lo, int):
                if r < 0:
                    continue
                vt = v_ref[r:r + QBLK, cols_v]
            else:
                vt = v_ref[pl.ds(pl.multiple_of(jnp.maximum(r, 0), QBLK), QBLK), cols_v]
            parts.append(_dot(p_ref[0:tq, QBLK * t:QBLK * (t + 1)], vt))
        return functools.reduce(jnp.add, parts)
    rows = pl.ds(pl.multiple_of(r_lo, QBLK), N_KBLK * QBLK)
    return _dot(p_ref[0:tq, :], v_ref[rows, cols_v])


def _attn_kernel(q_ref, k_ref, v_ref, g_ref, bias_ref, o_ref, *scratch, tq, nqb, pps, koff):
    step = pl.program_id(2)
    n_chain = 2 * nqb * pps
    s_refs = scratch[:n_chain]
    p_refs = scratch[n_chain:]
    lane = lax.broadcasted_iota(jnp.int32, (tq, LANES), 1)
    first = lane < DH

    full_from = max(-(-(BAND_ROWS - koff) // (QBLK * nqb)), 0)

    def compute(early):
        chains = [(qb, pp, hh) for qb in range(nqb) for pp in range(pps) for hh in range(2)]
        r_los = [(step * nqb + qb) * QBLK + koff - BAND_ROWS for qb in range(nqb)]
        if early and full_from == 1:
            r_los = [qb * QBLK + koff - BAND_ROWS if qb * QBLK + koff < BAND_ROWS else r
                     for qb, r in enumerate(r_los)]
        clamps = [early and (full_from > 1 or qb * QBLK + koff < BAND_ROWS) for qb in range(nqb)]
        sums, outs = {}, {}
        for step_c in range(n_chain + 2):
            if step_c < n_chain:
                qb, pp, hh = chains[step_c]
                cols = slice(pp * LANES, (pp + 1) * LANES)
                q2 = q_ref[qb * tq:(qb + 1) * tq, cols]
                qm = jnp.where(first if hh == 0 else jnp.logical_not(first), q2, jnp.zeros_like(q2))
                _attn_scores(qm, k_ref, cols, bias_ref, 2 * pp + hh, r_los[qb], s_refs[step_c],
                             tq=tq, clamp=clamps[qb])
            if 0 <= step_c - 1 < n_chain:
                sums[step_c - 1] = _attn_softmax(s_refs[step_c - 1], p_refs[step_c - 1], tq=tq)
            if 0 <= step_c - 2 < n_chain:
                c = step_c - 2
                qb, pp, hh = chains[c]
                outs[c] = _attn_values(p_refs[c], v_ref, slice(pp * LANES, (pp + 1) * LANES), r_los[qb],
                                       tq=tq, clamp=clamps[qb]) / sums[c]
        for qb in range(nqb):
            for pp in range(pps):
                rows = slice(qb * tq, (qb + 1) * tq)
                cols = slice(pp * LANES, (pp + 1) * LANES)
                c0 = 2 * (qb * pps + pp)
                o = jnp.where(first, outs[c0], outs[c0 + 1])
                g = g_ref[rows, cols].astype(F32)
                o_ref[rows, cols] = (o * (g * _sigmoid(g))).astype(BF16)

    if full_from == 0:
        compute(False)
    else:
        pl.when(step >= full_from)(lambda: compute(False))
        pl.when(step < full_from)(lambda: compute(True))


def _attention(q_arr, q_col, k_arr, k_col, v_arr, v_col, g_arr, g_col, bias, *,
               batch, q_rows, tq, nqb, pps, koff):
    k_rows = k_arr.shape[0] // batch
    ts = tq * nqb
    nq = q_rows // ts
    pairs = bias.shape[0] // 2
    assert q_rows % ts == 0 and (tq == QBLK or q_rows == tq) and pairs % pps == 0
    assert all(c % pps == 0 for c in (q_col, k_col, v_col, g_col))
    w = LANES * pps
    kern = functools.partial(_attn_kernel, tq=tq, nqb=nqb, pps=pps, koff=koff)
    return pl.pallas_call(
        kern,
        grid=(pairs // pps, batch, nq),
        in_specs=[
            pl.BlockSpec((ts, w), lambda p, b, i: (b * nq + i, q_col // pps + p)),
            pl.BlockSpec((k_rows, w), lambda p, b, i: (b, k_col // pps + p)),
            pl.BlockSpec((k_rows, w), lambda p, b, i: (b, v_col // pps + p)),
            pl.BlockSpec((ts, w), lambda p, b, i: (b * nq + i, g_col // pps + p)),
            pl.BlockSpec((2 * pps, QBLK, N_KBLK * QBLK), lambda p, b, i: (p, 0, 0)),
        ],
        out_specs=pl.BlockSpec((ts, w), lambda p, b, i: (b * nq + i, p)),
        out_shape=jax.ShapeDtypeStruct((batch * q_rows, pairs * LANES), BF16),
        scratch_shapes=([pltpu.VMEM((QBLK, N_KBLK * QBLK), F32)] * (2 * nqb * pps)
                        + [pltpu.VMEM((QBLK, N_KBLK * QBLK), BF16)] * (2 * nqb * pps)),
        compiler_params=pltpu.CompilerParams(
            dimension_semantics=("arbitrary", "arbitrary", "arbitrary"), vmem_limit_bytes=VMEM_LIMIT),
        name="band_attention",
    )(q_arr, k_arr, v_arr, g_arr, bias)


MERGE_COLS = 512


MERGE_ROWS = 256


def _merge_out_kernel(oa_ref, ob_ref, ma_ref, mb_ref, x_ref, wpa_ref, wpb_ref, wo_ref, gp_ref, y_ref, mg_ref):
    tm, d = x_ref.shape
    passes = [slice(r, r + MERGE_ROWS) for r in range(0, tm, MERGE_ROWS)]
    for rows in passes:
        for n in range(d // MERGE_COLS):
            cols = slice(n * MERGE_COLS, (n + 1) * MERGE_COLS)
            pa = _dot(oa_ref[rows, :], wpa_ref[:, cols])
            pb = _dot(ob_ref[rows, :], wpb_ref[:, cols])
            merged = (_sigmoid(ma_ref[rows, cols].astype(F32)) * pa
                      + _sigmoid(mb_ref[rows, cols].astype(F32)) * pb)
            mg_ref[rows, cols] = merged.astype(BF16)
    for rows in passes:
        hid = _dot(mg_ref[rows, :], wo_ref[...])
        ms = jnp.mean(hid * hid, axis=-1, keepdims=True)
        y_ref[rows, :] = x_ref[rows, :] + (hid * lax.rsqrt(ms + EPS)) * gp_ref[...]


def _merge_out(o_a, o_b, z, x2, wpa_bf, wpb_bf, wo_bf, norm_post, *, tm=512):
    m, d = x2.shape
    d_a = o_a.shape[1]
    d_b = o_b.shape[1]
    ma_col = (4 * d_a + 4 * d_b) // d
    resident = dict(pipeline_mode=pl.Buffered(1))
    return pl.pallas_call(
        _merge_out_kernel,
        grid=(m // tm,),
        in_specs=[
            pl.BlockSpec((tm, d_a), lambda i: (i, 0)),
            pl.BlockSpec((tm, d_b), lambda i: (i, 0)),
            pl.BlockSpec((tm, d), lambda i: (i, ma_col)),
            pl.BlockSpec((tm, d), lambda i: (i, ma_col + 1)),
            pl.BlockSpec((tm, d), lambda i: (i, 0)),
            pl.BlockSpec((d_a, d), lambda i: (0, 0), **resident),
            pl.BlockSpec((d_b, d), lambda i: (0, 0), **resident),
            pl.BlockSpec((d, d), lambda i: (0, 0), **resident),
            pl.BlockSpec((1, d), lambda i: (0, 0)),
        ],
        out_specs=pl.BlockSpec((tm, d), lambda i: (i, 0)),
        out_shape=jax.ShapeDtypeStruct((m, d), F32),
        scratch_shapes=[pltpu.VMEM((tm, d), BF16)],
        compiler_params=pltpu.CompilerParams(
            dimension_semantics=("arbitrary",), vmem_limit_bytes=VMEM_LIMIT),
        name="merge_out",
    )(o_a, o_b, z, z, x2, wpa_bf, wpb_bf, wo_bf, norm_post.reshape(1, d))


def kernel(x_prompt, x_sample, state_hgrn, cache_k, cache_v, norm_pre, w_in, lb_logits, gnorm_a,
           rel_bias, w_proj_a, w_proj_b, w_out, norm_post):
    depth = w_in.shape[0]
    assert depth == 1, "single-layer step"
    bp, tp, d = x_prompt.shape
    bs, ts, _ = x_sample.shape
    heads_a = state_hgrn.shape[2]
    heads_b = cache_k.shape[3]
    kv_rows = cache_k.shape[2]
    d_a = heads_a * DK
    d_b = heads_b * DH
    assert ts == CHUNK and kv_rows == BAND_ROWS and PAST_LEN % QBLK == 0 and tp % QBLK == 0

    w_in_bf = w_in[0].astype(BF16)
    wpa_bf = w_proj_a[0].astype(BF16)
    wpb_bf = w_proj_b[0].astype(BF16)
    wo_bf = w_out[0].astype(BF16)
    bias = _bias_table(rel_bias[0])
    col_b = 4 * d_a // LANES
    grp_b = d_b // LANES

    xp2 = x_prompt.reshape(bp * tp, d)
    rows_p = min(BAND_ROWS, tp)
    z_p, k_p, v_p = _in_proj(xp2, norm_pre[0], w_in_bf, rows_per_batch=tp, kv_rows=rows_p)
    oa_p, s_p = _hgrn(z_p, lb_logits, gnorm_a[0], jnp.zeros((bp, heads_a, DK, DK), F32),
                      batch=bp, rows_per_batch=tp, tc=4096, hps=1)
    ob_p = _attention(z_p, col_b, z_p, col_b + grp_b, z_p, col_b + 2 * grp_b, z_p, col_b + 3 * grp_b,
                      bias, batch=bp, q_rows=tp, tq=QBLK, nqb=8, pps=1, koff=0)
    y_p = _merge_out(oa_p, ob_p, z_p, xp2, wpa_bf, wpb_bf, wo_bf, norm_post[0])

    xs2 = x_sample.reshape(bs * ts, d)
    z_s, k_s, v_s = _in_proj(xs2, norm_pre[0], w_in_bf, rows_per_batch=ts, kv_rows=ts)
    oa_s, s_s = _hgrn(z_s, lb_logits, gnorm_a[0], state_hgrn[0],
                      batch=bs, rows_per_batch=ts, tc=ts, hps=heads_a)
    pad = jnp.zeros((bs, QBLK - ts, d_b), BF16)
    k_all = jnp.concatenate([cache_k[0].astype(BF16).reshape(bs, kv_rows, d_b),
                             k_s.reshape(bs, ts, d_b).astype(BF16), pad], axis=1)
    v_all = jnp.concatenate([cache_v[0].astype(BF16).reshape(bs, kv_rows, d_b),
                             v_s.reshape(bs, ts, d_b).astype(BF16), pad], axis=1)
    k_rows = kv_rows + QBLK
    ob_s = _attention(z_s, col_b, k_all.reshape(bs * k_rows, d_b), 0, v_all.reshape(bs * k_rows, d_b), 0,
                      z_s, col_b + 3 * grp_b, bias, batch=bs, q_rows=ts, tq=ts, nqb=1, pps=4, koff=kv_rows)
    y_s = _merge_out(oa_s, ob_s, z_s, xs2, wpa_bf, wpb_bf, wo_bf, norm_post[0])

    return (
        y_p.reshape(bp, tp, d),
        y_s.reshape(bs, ts, d),
        s_p[None],
        k_p.reshape(1, bp, rows_p, heads_b, DH),
        v_p.reshape(1, bp, rows_p, heads_b, DH),
        s_s[None],
        k_s.reshape(1, bs, ts, heads_b, DH),
        v_s.reshape(1, bs, ts, heads_b, DH),
    )
```
